```python
import math
import jax, jax.numpy as jnp
from jax import lax
import numpy as np

D_MODEL = 1024
BATCH = 2
SEQ = 16384
DEPTH = 2
DEC_BATCH = 2
DEC_SEQ = 8192
PAST_LEN = 128

HEAD_DIM = 64
DIFF_HEADS = 4
DIFF_DH = 64
SWA_HEADS = 8
SWA_KV_HEADS = 2
SWA_WINDOW = 128
SWA_BLOCK = 128
MLA_HEADS = 8
MLA_Q_RANK = 256
MLA_KV_RANK = 128
MLA_NOPE = 64
MLA_ROPE = 32
MLA_VDIM = 64
NA_HEADS = 8
NA_DH = 64
NA_KR_MAX = 8
NA_KC = 16
NA_QB = 16
NA_KB = 32
GRID_W = 64
N_EXPERTS = 16
EC_CAPACITY_FACTOR = 2
D_EXPERT = 2816

ROPE_THETA = 10000.0
Q_BLOCK = 128
DEEPNORM_ALPHA = (2 * DEPTH) ** 0.25
DEEPNORM_BETA = (8 * DEPTH) ** -0.25
N_EVEN = (DEPTH + 1) // 2
N_ODD = DEPTH // 2
LN_EPS = 1e-5
RMS_EPS = 1e-6
NEG_INF = -1e30

EVEN_SPLIT = (DIFF_HEADS * 2 * DIFF_DH, DIFF_HEADS * 2 * DIFF_DH, DIFF_HEADS * 2 * DIFF_DH,
              SWA_HEADS * HEAD_DIM, SWA_KV_HEADS * HEAD_DIM, SWA_KV_HEADS * HEAD_DIM)
EVEN_IN = sum(EVEN_SPLIT)
EVEN_OUT = DIFF_HEADS * 2 * DIFF_DH + SWA_HEADS * HEAD_DIM
ODD_SPLIT = (MLA_Q_RANK, MLA_KV_RANK, MLA_ROPE,
             NA_HEADS * NA_DH, NA_HEADS * NA_DH, NA_HEADS * NA_DH)
ODD_IN = sum(ODD_SPLIT)
ODD_OUT = MLA_HEADS * MLA_VDIM + NA_HEADS * NA_DH

kernel_name = 'hybrid_diff_swa_mla_na_ec_encoder'


def split_cols(x, sizes):
    outs, off = [], 0
    for n in sizes:
        outs.append(x[..., off:off + n])
        off += n
    return outs


def layer_norm(x, g, b):
    xf = x.astype(jnp.float32)
    mu = jnp.mean(xf, -1, keepdims=True)
    var = jnp.mean(jnp.square(xf - mu), -1, keepdims=True)
    y = (xf - mu) * lax.rsqrt(var + LN_EPS) * g.astype(jnp.float32) + b.astype(jnp.float32)
    return y.astype(x.dtype)


def rms_norm(x, g):
    xf = x.astype(jnp.float32)
    y = xf * lax.rsqrt(jnp.mean(jnp.square(xf), -1, keepdims=True) + RMS_EPS) * g.astype(jnp.float32)
    return y.astype(x.dtype)


def rope_tables(seq, dim):
    inv = 1.0 / (ROPE_THETA ** (jnp.arange(0, dim, 2, dtype=jnp.float32) / dim))
    ang = jnp.arange(seq, dtype=jnp.float32)[:, None] * inv[None, :]
    return jnp.cos(ang), jnp.sin(ang)


def apply_rope(x, cos, sin):
    half = x.shape[-1] // 2
    x1 = x[..., :half].astype(jnp.float32)
    x2 = x[..., half:].astype(jnp.float32)
    c = cos[None, :, None, :]
    s = sin[None, :, None, :]
    return jnp.concatenate([x1 * c - x2 * s, x2 * c + x1 * s], -1).astype(x.dtype)


def diff_attention(q, k, v, lam, subln, lam_init):
    B, S, H, _, dh = q.shape
    nb = S // Q_BLOCK
    scale = dh ** -0.5
    lf = lam.astype(jnp.float32)
    lmbda = jnp.exp(jnp.sum(lf[0] * lf[1])) - jnp.exp(jnp.sum(lf[2] * lf[3])) + lam_init
    qb = q.reshape(B, nb, Q_BLOCK, H, 2, dh).swapaxes(0, 1)

    def block(qblk):
        s = jnp.einsum('bqhmd,bkhmd->bhmqk', qblk, k).astype(jnp.float32) * scale
        p = jax.nn.softmax(s, axis=-1)
        a = p[:, :, 0] - lmbda * p[:, :, 1]
        return jnp.einsum('bhqk,bkhd->bqhd', a.astype(v.dtype), v)

    o = lax.map(block, qb).swapaxes(0, 1).reshape(B, S, H, 2 * dh)
    o = rms_norm(o, subln) * (1.0 - lam_init)
    return o.reshape(B, S, H * 2 * dh)


def window_gqa(q, k, v, sink):
    B, S, H, dh = q.shape
    G = k.shape[2]
    R = H // G
    W = SWA_BLOCK
    nb = S // W
    pad = ((0, 0), (W, W), (0, 0), (0, 0))
    kp = jnp.pad(k, pad).reshape(B, nb + 2, W, G, dh)
    vp = jnp.pad(v, pad).reshape(B, nb + 2, W, G, dh)
    kw = jnp.concatenate([kp[:, :-2], kp[:, 1:-1], kp[:, 2:]], axis=2)
    vw = jnp.concatenate([vp[:, :-2], vp[:, 1:-1], vp[:, 2:]], axis=2)
    qb = q.reshape(B, nb, W, G, R, dh)
    s = jnp.einsum('bnqgrd,bnkgd->bngrqk', qb, kw).astype(jnp.float32) * dh ** -0.5
    blk = jnp.arange(nb)[:, None] * W
    qpos = blk + jnp.arange(W)[None, :]
    kpos = blk - W + jnp.arange(3 * W)[None, :]
    valid = ((kpos[:, None, :] >= 0) & (kpos[:, None, :] < S)
             & (jnp.abs(qpos[:, :, None] - kpos[:, None, :]) <= SWA_WINDOW))
    s = jnp.where(valid[None, :, None, None], s, NEG_INF)
    sink_l = sink.astype(jnp.float32).reshape(G, R)[None, None, :, :, None, None]
    m = jnp.maximum(jnp.max(s, -1, keepdims=True), sink_l)
    e = jnp.exp(s - m)
    p = e / (jnp.sum(e, -1, keepdims=True) + jnp.exp(sink_l - m))
    o = jnp.einsum('bngrqk,bnkgd->bnqgrd', p.astype(v.dtype), vw)
    return o.reshape(B, S, H * dh)


def mla_attention(c_q, c_kv, k_rope, q_norm, w_uq, kv_norm, w_ukv, cos, sin):
    B, S, _ = c_q.shape
    H = MLA_HEADS
    q = (rms_norm(c_q, q_norm) @ w_uq).reshape(B, S, H, MLA_NOPE + MLA_ROPE)
    q_nope, q_pe = q[..., :MLA_NOPE], apply_rope(q[..., MLA_NOPE:], cos, sin)
    kv = (rms_norm(c_kv, kv_norm) @ w_ukv).reshape(B, S, H, MLA_NOPE + MLA_VDIM)
    k_nope, v = kv[..., :MLA_NOPE], kv[..., MLA_NOPE:]
    k_pe = apply_rope(k_rope[:, :, None, :], cos, sin)[:, :, 0]
    scale = (MLA_NOPE + MLA_ROPE) ** -0.5
    nb = S // Q_BLOCK
    qn_b = q_nope.reshape(B, nb, Q_BLOCK, H, MLA_NOPE).swapaxes(0, 1)
    qp_b = q_pe.reshape(B, nb, Q_BLOCK, H, MLA_ROPE).swapaxes(0, 1)

    def block(args):
        qn, qp = args
        s = (jnp.einsum('bqhd,bkhd->bhqk', qn, k_nope)
             + jnp.einsum('bqhd,bkd->bhqk', qp, k_pe)).astype(jnp.float32) * scale
        p = jax.nn.softmax(s, axis=-1)
        return jnp.einsum('bhqk,bkhd->bqhd', p.astype(v.dtype), v)

    o = lax.map(block, (qn_b, qp_b)).swapaxes(0, 1)
    return o.reshape(B, S, H * MLA_VDIM)


def neighbourhood_attention(q, k, v, rpb):
    B, S, H, dh = q.shape
    rows = S // GRID_W
    kr = min(NA_KR_MAX, rows)
    nj = GRID_W // NA_QB
    qg = q.reshape(B, rows, nj, NA_QB, H, dh)
    kg = k.reshape(B, rows, GRID_W, H, dh)
    vg = v.reshape(B, rows, GRID_W, H, dh)
    r = jnp.arange(rows)
    row_idx = jnp.clip(r - kr // 2, 0, rows - kr)[:, None] + jnp.arange(kr)[None, :]
    j = jnp.arange(nj)
    col_idx = jnp.clip(j * NA_QB - NA_KC // 2, 0, GRID_W - NA_KB)[:, None] + jnp.arange(NA_KB)[None, :]
    ri = row_idx[:, None, :, None]
    ci = col_idx[None, :, None, :]
    kw = kg[:, ri, ci]
    vw = vg[:, ri, ci]
    s = jnp.einsum('brjqhd,brjkchd->brjhqkc', qg, kw).astype(jnp.float32) * dh ** -0.5
    qcol = j[:, None] * NA_QB + jnp.arange(NA_QB)[None, :]
    qcs = jnp.clip(qcol - NA_KC // 2, 0, GRID_W - NA_KC)
    kcol = col_idx[:, None, :]
    valid = (kcol >= qcs[:, :, None]) & (kcol < qcs[:, :, None] + NA_KC)
    dr = row_idx - r[:, None] + (NA_KR_MAX - 1)
    dc = jnp.clip(kcol - qcol[:, :, None] + (NA_KC - 1), 0, 2 * NA_KC - 2)
    bias = rpb[:, dr[:, None, None, :, None], dc[None, :, :, None, :]]
    s = s + bias.transpose(1, 2, 0, 3, 4, 5)[None].astype(jnp.float32)
    s = jnp.where(valid[None, None, :, None, :, None, :], s, NEG_INF)
    p = jax.nn.softmax(s.reshape(s.shape[:5] + (kr * NA_KB,)), axis=-1).reshape(s.shape)
    o = jnp.einsum('brjhqkc,brjkchd->brjqhd', p.astype(v.dtype), vw)
    return o.reshape(B, S, H * dh)


def even_mixer(h, w_in, lam, subln, sink, w_out, cos, sin, lam_init):
    B, S, _ = h.shape
    qa, ka, va, qb, kb, vb = split_cols(h @ w_in, EVEN_SPLIT)
    qa = apply_rope(qa.reshape(B, S, DIFF_HEADS * 2, DIFF_DH), cos, sin).reshape(B, S, DIFF_HEADS, 2, DIFF_DH)
    ka = apply_rope(ka.reshape(B, S, DIFF_HEADS * 2, DIFF_DH), cos, sin).reshape(B, S, DIFF_HEADS, 2, DIFF_DH)
    va = va.reshape(B, S, DIFF_HEADS, 2 * DIFF_DH)
    oa = diff_attention(qa, ka, va, lam, subln, lam_init)
    qb = apply_rope(qb.reshape(B, S, SWA_HEADS, HEAD_DIM), cos, sin)
    kb = apply_rope(kb.reshape(B, S, SWA_KV_HEADS, HEAD_DIM), cos, sin)
    vb = vb.reshape(B, S, SWA_KV_HEADS, HEAD_DIM)
    ob = window_gqa(qb, kb, vb, sink)
    return jnp.concatenate([oa, ob], -1) @ w_out


def odd_mixer(h, w_in, q_norm, w_uq, kv_norm, w_ukv, rpb, w_out, cos, sin):
    B, S, _ = h.shape
    c_q, c_kv, k_rope, qn, kn, vn = split_cols(h @ w_in, ODD_SPLIT)
    oc = mla_attention(c_q, c_kv, k_rope, q_norm, w_uq, kv_norm, w_ukv, cos, sin)
    shp = (B, S, NA_HEADS, NA_DH)
    od = neighbourhood_attention(qn.reshape(shp), kn.reshape(shp), vn.reshape(shp), rpb)
    return jnp.concatenate([oc, od], -1) @ w_out


def expert_choice_ffn(h, router_w, w_gate, w_up, w_down):
    B, S, D = h.shape
    n = B * S
    cap = EC_CAPACITY_FACTOR * n // N_EXPERTS
    xf = h.reshape(n, D)
    aff = jax.nn.softmax((xf @ router_w).astype(jnp.float32), axis=-1)
    gates, idx = lax.top_k(aff.T, cap)
    xe = xf[idx]
    a = jnp.einsum('ecd,edf->ecf', xe, w_gate)
    u = jnp.einsum('ecd,edf->ecf', xe, w_up)
    y = jnp.einsum('ecf,efd->ecd', jax.nn.silu(a) * u, w_down) * gates[..., None].astype(h.dtype)
    out = jnp.zeros_like(xf).at[idx.reshape(-1)].add(y.reshape(-1, D))
    return out.reshape(B, S, D)


def trunk(x, c, w_in_even, diff_lambda, diff_subln, swa_sink, w_out_even,
          w_in_odd, mla_q_norm, mla_w_uq, mla_kv_norm, mla_w_ukv, na_rpb, w_out_odd,
          ada_w, ada_b, ln_g, ln_b, router_w, exp_w_gate, exp_w_up, exp_w_down):
    S = x.shape[1]
    cos_h, sin_h = rope_tables(S, HEAD_DIM)
    cos_r, sin_r = rope_tables(S, MLA_ROPE)
    c_act = jax.nn.silu(c)
    for l in range(DEPTH):
        mod = (c_act @ ada_w[l] + ada_b[l])[:, None, :]
        sh1, sc1, g1, sh2, sc2, g2 = split_cols(mod, (D_MODEL,) * 6)
        h = x * (1 + sc1) + sh1
        if l % 2 == 0:
            e = l // 2
            lam_init = 0.8 - 0.6 * math.exp(-0.3 * l)
            y = even_mixer(h, w_in_even[e], diff_lambda[e], diff_subln[e], swa_sink[e], w_out_even[e],
                           cos_h, sin_h, lam_init)
        else:
            o = l // 2
            y = odd_mixer(h, w_in_odd[o], mla_q_norm[o], mla_w_uq[o], mla_kv_norm[o], mla_w_ukv[o],
                          na_rpb[o], w_out_odd[o], cos_r, sin_r)
        x = layer_norm(DEEPNORM_ALPHA * x + g1 * y, ln_g[l, 0], ln_b[l, 0])
        h = x * (1 + sc2) + sh2
        y = expert_choice_ffn(h, router_w[l], exp_w_gate[l], exp_w_up[l], exp_w_down[l])
        x = layer_norm(DEEPNORM_ALPHA * x + g2 * y, ln_g[l, 1], ln_b[l, 1])
    return x


def setup_inputs(seed: int = 0) -> dict:
    key = jax.random.key(seed)
    ks = jax.random.split(key, 24)
    D = D_MODEL
    nrm = lambda k, shp, sc: jax.random.normal(k, shp, jnp.float32) * sc
    return {
        'x_prompt': nrm(ks[0], (BATCH, SEQ, D), 1.0),
        'x_sample': nrm(ks[1], (DEC_BATCH, DEC_SEQ, D), 1.0),
        'c_prompt': nrm(ks[2], (BATCH, D), 1.0),
        'c_sample': nrm(ks[3], (DEC_BATCH, D), 1.0),
        'w_in_even': nrm(ks[4], (N_EVEN, D, EVEN_IN), D ** -0.5),
        'diff_lambda': nrm(ks[5], (N_EVEN, 4, DIFF_DH), 0.1),
        'diff_subln': 1.0 + nrm(ks[6], (N_EVEN, 2 * DIFF_DH), 0.02),
        'swa_sink': nrm(ks[7], (N_EVEN, SWA_HEADS), 0.5),
        'w_out_even': nrm(ks[8], (N_EVEN, EVEN_OUT, D), EVEN_OUT ** -0.5 * DEEPNORM_BETA),
        'w_in_odd': nrm(ks[9], (N_ODD, D, ODD_IN), D ** -0.5),
        'mla_q_norm': 1.0 + nrm(ks[10], (N_ODD, MLA_Q_RANK), 0.02),
        'mla_w_uq': nrm(ks[11], (N_ODD, MLA_Q_RANK, MLA_HEADS * (MLA_NOPE + MLA_ROPE)), MLA_Q_RANK ** -0.5),
        'mla_kv_norm': 1.0 + nrm(ks[12], (N_ODD, MLA_KV_RANK), 0.02),
        'mla_w_ukv': nrm(ks[13], (N_ODD, MLA_KV_RANK, MLA_HEADS * (MLA_NOPE + MLA_VDIM)), MLA_KV_RANK ** -0.5),
        'na_rpb': nrm(ks[14], (N_ODD, NA_HEADS, 2 * NA_KR_MAX - 1, 2 * NA_KC - 1), 0.1),
        'w_out_odd': nrm(ks[15], (N_ODD, ODD_OUT, D), ODD_OUT ** -0.5 * DEEPNORM_BETA),
        'ada_w': nrm(ks[16], (DEPTH, D, 6 * D), D ** -0.5),
        'ada_b': nrm(ks[17], (DEPTH, 6 * D), 0.02),
        'ln_g': 1.0 + nrm(ks[18], (DEPTH, 2, D), 0.02),
        'ln_b': nrm(ks[19], (DEPTH, 2, D), 0.02),
        'router_w': nrm(ks[20], (DEPTH, D, N_EXPERTS), D ** -0.5),
        'exp_w_gate': nrm(ks[21], (DEPTH, N_EXPERTS, D, D_EXPERT), D ** -0.5),
        'exp_w_up': nrm(ks[22], (DEPTH, N_EXPERTS, D, D_EXPERT), D ** -0.5),
        'exp_w_down': nrm(ks[23], (DEPTH, N_EXPERTS, D_EXPERT, D), D_EXPERT ** -0.5 * DEEPNORM_BETA),
    }


def reference(x_prompt, x_sample, c_prompt, c_sample, w_in_even, diff_lambda, diff_subln, swa_sink,
              w_out_even, w_in_odd, mla_q_norm, mla_w_uq, mla_kv_norm, mla_w_ukv, na_rpb, w_out_odd,
              ada_w, ada_b, ln_g, ln_b, router_w, exp_w_gate, exp_w_up, exp_w_down):
    y_prompt = trunk(x_prompt, c_prompt, w_in_even, diff_lambda, diff_subln, swa_sink, w_out_even,
                     w_in_odd, mla_q_norm, mla_w_uq, mla_kv_norm, mla_w_ukv, na_rpb, w_out_odd,
                     ada_w, ada_b, ln_g, ln_b, router_w, exp_w_gate, exp_w_up, exp_w_down)
    y_sample = trunk(x_sample, c_sample, w_in_even, diff_lambda, diff_subln, swa_sink, w_out_even,
                     w_in_odd, mla_q_norm, mla_w_uq, mla_kv_norm, mla_w_ukv, na_rpb, w_out_odd,
                     ada_w, ada_b, ln_g, ln_b, router_w, exp_w_gate, exp_w_up, exp_w_down)
    return (y_prompt, y_sample)
```

```python
import functools
import math

import jax
import jax.numpy as jnp
from jax import lax
from jax.experimental import pallas as pl
from jax.experimental.pallas import tpu as pltpu

F32, BF16, I32 = jnp.float32, jnp.bfloat16, jnp.int32

D_MODEL = 1024
DEPTH = 2
DIFF_HEADS, DIFF_DH = 4, 64
SWA_HEADS, SWA_KV_HEADS, SWA_WINDOW, HEAD_DIM = 8, 2, 128, 64
MLA_HEADS, MLA_Q_RANK, MLA_KV_RANK, MLA_NOPE, MLA_ROPE, MLA_VDIM = 8, 256, 128, 64, 32, 64
NA_HEADS, NA_DH, NA_KR, NA_KC, GRID_W = 8, 64, 8, 16, 64
N_EXPERTS, EC_CAPACITY_FACTOR, D_EXPERT = 16, 2, 2816
ROPE_THETA = 10000.0
DEEPNORM_ALPHA = (2 * DEPTH) ** 0.25
LN_EPS, RMS_EPS, NEG_INF = 1e-5, 1e-6, -1e30

LANES = 128
MXU_COLS = 256
VMEM_LIMIT_V7X = 56 * 1024 * 1024
NA_QROWS, NA_KROWS = 8, 16
NT_DIMS = (((1,), (1,)), ((), ()))


def _cparams(sem, vmem=None):
    return pltpu.CompilerParams(dimension_semantics=sem, vmem_limit_bytes=vmem)


def _dot(a, b):
    return jnp.dot(a, b, preferred_element_type=F32)


def _dot_nt(a, b):
    return lax.dot_general(a, b, NT_DIMS, preferred_element_type=F32)


def _layer_norm(z, g, b):
    mu = jnp.mean(z, axis=-1, keepdims=True)
    zc = z - mu
    var = jnp.mean(zc * zc, axis=-1, keepdims=True)
    return zc * lax.rsqrt(var + LN_EPS) * g + b


def _rms_norm(x, g):
    return x * lax.rsqrt(jnp.mean(x * x, axis=-1, keepdims=True) + RMS_EPS) * g


def _rope128(r, cos, sin, half):
    lane = lax.broadcasted_iota(I32, r.shape, 1)
    first = (lane % (2 * half)) < half
    rot = jnp.where(first, pltpu.roll(r, LANES - half, 1), pltpu.roll(r, half, 1))
    return r * cos + rot * sin


def _lane_rep(m, width):
    return m if width == LANES else jnp.concatenate([m] * (width // LANES), axis=1)


def _mod_kernel(c_ref, w_ref, b_ref, o_ref):
    c = c_ref[...]
    a = (c * jax.nn.sigmoid(c)).astype(BF16)
    o_ref[0] = _dot(a, w_ref[0].astype(BF16)) + b_ref[0]


def _modulation(c_all, ada_w, ada_b):
    nb = 1536
    n6 = ada_w.shape[2]
    return pl.pallas_call(
        _mod_kernel,
        out_shape=jax.ShapeDtypeStruct((DEPTH, 8, n6), F32),
        grid=(DEPTH, n6 // nb),
        in_specs=[pl.BlockSpec((8, D_MODEL), lambda l, j: (0, 0)),
                  pl.BlockSpec((1, D_MODEL, nb), lambda l, j: (l, 0, j)),
                  pl.BlockSpec((1, 1, nb), lambda l, j: (l, 0, j))],
        out_specs=pl.BlockSpec((1, 8, nb), lambda l, j: (l, 0, j)),
        compiler_params=_cparams(("arbitrary", "arbitrary"), 40 * 1024 * 1024),
        name="adaln_mod",
    )(c_all, ada_w, ada_b.reshape(DEPTH, 1, n6))


EVEN_SEGS = (("qa", 512, True), ("ka", 512, True), ("va", 512, False), ("qb", 1024, True),
             ("kb", 128, True), ("vb", 128, False), ("vbs", 128, False))


def _prep_w_in_even(w):
    scale = HEAD_DIM ** -0.5
    qa, ka, va = w[:, 0:512] * scale, w[:, 512:1024], w[:, 1024:1536]
    qb, kb, vb = w[:, 1536:2048] * scale, w[:, 2048:2176], w[:, 2176:2304]
    rep = SWA_HEADS // SWA_KV_HEADS
    z = jnp.zeros((w.shape[0], 64), w.dtype)
    qbp = []
    for h in range(SWA_HEADS):
        wh = qb[:, 64 * h:64 * h + 64]
        qbp += [wh, z] if h // rep == 0 else [z, wh]
    vbs = jnp.concatenate([vb[:, 64:], vb[:, :64]], axis=1)
    return jnp.concatenate([qa, ka, va] + qbp + [kb, vb, vbs], axis=1).astype(BF16)


def _inproj_even_kernel(x_ref, sc_ref, sh_ref, w_ref, cos_ref, sin_ref, *out_refs):
    h = (x_ref[0] * (1.0 + sc_ref[0]) + sh_ref[0]).astype(BF16)
    cos, sin = cos_ref[...], sin_ref[...]
    off = 0
    for (_, width, rope), o_ref in zip(EVEN_SEGS, out_refs):
        for c0 in range(0, width, MXU_COLS):
            cw = min(MXU_COLS, width - c0)
            r = _dot(h, w_ref[:, off + c0:off + c0 + cw])
            for j in range(cw // LANES):
                rj = r[:, j * LANES:(j + 1) * LANES]
                if rope:
                    rj = _rope128(rj, cos, sin, HEAD_DIM // 2)
                o_ref[0, :, c0 + j * LANES:c0 + (j + 1) * LANES] = rj.astype(BF16)
        off += width


def _inproj_even(x, sc, sh, w, cos, sin, tile):
    B, S, _ = x.shape
    ntot = w.shape[1]
    row = lambda b, i: (b, i, 0)
    vec = lambda b, i: (b, 0, 0)
    return pl.pallas_call(
        _inproj_even_kernel,
        out_shape=[jax.ShapeDtypeStruct((B, S, wd), BF16) for _, wd, _ in EVEN_SEGS],
        grid=(B, S // tile),
        in_specs=[pl.BlockSpec((1, tile, D_MODEL), row),
                  pl.BlockSpec((1, 1, D_MODEL), vec), pl.BlockSpec((1, 1, D_MODEL), vec),
                  pl.BlockSpec((D_MODEL, ntot), lambda b, i: (0, 0)),
                  pl.BlockSpec((tile, LANES), lambda b, i: (i, 0)),
                  pl.BlockSpec((tile, LANES), lambda b, i: (i, 0))],
        out_specs=[pl.BlockSpec((1, tile, wd), row) for _, wd, _ in EVEN_SEGS],
        compiler_params=_cparams(("arbitrary", "arbitrary"), 48 * 1024 * 1024),
        name="inproj_even",
    )(x, sc, sh, w, cos, sin)


def _diff_attn_kernel(q_ref, k_ref, v_ref, lam_ref, subln_ref, o_ref, m_ref, l_ref, acc_ref, *, tk, lam_init):
    S = k_ref.shape[1]
    q = q_ref[0]
    lane = lax.broadcasted_iota(I32, q.shape, 1)
    zero = jnp.zeros_like(q)
    qs = (jnp.where(lane < DIFF_DH, q, zero), jnp.where(lane >= DIFF_DH, q, zero))
    m_ref[...] = jnp.full(m_ref.shape, NEG_INF, F32)
    l_ref[...] = jnp.zeros(l_ref.shape, F32)
    acc_ref[...] = jnp.zeros(acc_ref.shape, F32)

    def body(j, carry):
        off = pl.multiple_of(j * tk, tk)
        k = k_ref[0, pl.ds(off, tk), :]
        v = v_ref[0, pl.ds(off, tk), :]
        for i in range(2):
            s = _dot_nt(qs[i], k)
            m_prev = m_ref[i]
            m_new = jnp.maximum(m_prev, jnp.max(s, axis=1, keepdims=True))
            p = jnp.exp(s - _lane_rep(m_new, tk))
            alpha = jnp.exp(m_prev - m_new)
            l_ref[i] = alpha * l_ref[i] + jnp.sum(p, axis=1, keepdims=True)
            acc_ref[i] = alpha * acc_ref[i] + _dot(p.astype(BF16), v)
            m_ref[i] = m_new
        return carry

    lax.fori_loop(0, S // tk, body, 0)
    lf = lam_ref[...]
    lam = (jnp.exp(jnp.sum(lf[0:1] * lf[1:2], axis=1, keepdims=True))
           - jnp.exp(jnp.sum(lf[2:3] * lf[3:4], axis=1, keepdims=True)) + lam_init)
    o = acc_ref[0] / l_ref[0] - lam * (acc_ref[1] / l_ref[1])
    o_ref[0] = (_rms_norm(o, subln_ref[...]) * (1.0 - lam_init)).astype(BF16)


def _diff_attention(qa, ka, va, lam, subln, lam_init, tq, tk):
    B, S, _ = qa.shape
    kern = functools.partial(_diff_attn_kernel, tk=tk, lam_init=lam_init)
    return pl.pallas_call(
        kern,
        out_shape=jax.ShapeDtypeStruct((B, S, DIFF_HEADS * LANES), BF16),
        grid=(B, DIFF_HEADS, S // tq),
        in_specs=[pl.BlockSpec((1, tq, LANES), lambda b, h, i: (b, i, h)),
                  pl.BlockSpec((1, S, LANES), lambda b, h, i: (b, 0, h)),
                  pl.BlockSpec((1, S, LANES), lambda b, h, i: (b, 0, h)),
                  pl.BlockSpec((4, DIFF_DH), lambda b, h, i: (0, 0)),
                  pl.BlockSpec((1, LANES), lambda b, h, i: (0, 0))],
        out_specs=pl.BlockSpec((1, tq, LANES), lambda b, h, i: (b, i, h)),
        scratch_shapes=[pltpu.VMEM((2, tq, LANES), F32)] * 3,
        compiler_params=_cparams(("arbitrary",) * 3, 48 * 1024 * 1024),
        name="diff_attn",
    )(qa, ka, va, lam, subln.reshape(1, LANES))


def _window_kernel(q_ref, kp_ref, kc_ref, kn_ref, vp_ref, vc_ref, vn_ref, wp_ref, wc_ref, wn_ref,
                   sink_ref, o_ref, *, S):
    W = SWA_WINDOW
    i = pl.program_id(1)
    k = jnp.concatenate([kp_ref[0], kc_ref[0], kn_ref[0]], axis=0)
    v = jnp.concatenate([vp_ref[0], vc_ref[0], vn_ref[0]], axis=0)
    vs = jnp.concatenate([wp_ref[0], wc_ref[0], wn_ref[0]], axis=0)
    qpos = i * W + lax.broadcasted_iota(I32, (W, 3 * W), 0)
    kpos = i * W - W + lax.broadcasted_iota(I32, (W, 3 * W), 1)
    valid = (kpos >= 0) & (kpos < S) & (jnp.abs(qpos - kpos) <= SWA_WINDOW)
    lane = lax.broadcasted_iota(I32, (W, LANES), 1)
    rep = SWA_HEADS // SWA_KV_HEADS
    for pair in range(SWA_HEADS // 2):
        outs = []
        for hh in range(2):
            h = 2 * pair + hh
            g = h // rep
            s = _dot_nt(q_ref[0, :, h * LANES:(h + 1) * LANES], k)
            s = jnp.where(valid, s, NEG_INF)
            sk = sink_ref[0:1, h:h + 1]
            m = jnp.maximum(jnp.max(s, axis=1, keepdims=True), sk)
            e = jnp.exp(s - m)
            p = e / (jnp.sum(e, axis=1, keepdims=True) + jnp.exp(sk - m))
            outs.append(_dot(p.astype(BF16), v if g == hh else vs))
        o_ref[0, :, pair * LANES:(pair + 1) * LANES] = jnp.where(lane < HEAD_DIM, outs[0], outs[1]).astype(BF16)


def _window_attention(qbp, kb, vb, vbs, sink):
    B, S, _ = qbp.shape
    W = SWA_WINDOW
    nb = S // W
    prev = lambda b, i: (b, jnp.maximum(i - 1, 0), 0)
    cur = lambda b, i: (b, i, 0)
    nxt = lambda b, i: (b, jnp.minimum(i + 1, nb - 1), 0)
    kv_specs = [pl.BlockSpec((1, W, LANES), f) for f in (prev, cur, nxt)]
    return pl.pallas_call(
        functools.partial(_window_kernel, S=S),
        out_shape=jax.ShapeDtypeStruct((B, S, SWA_HEADS * HEAD_DIM), BF16),
        grid=(B, nb),
        in_specs=[pl.BlockSpec((1, W, SWA_HEADS * LANES), cur)] + kv_specs * 3
                 + [pl.BlockSpec((1, SWA_HEADS), lambda b, i: (0, 0))],
        out_specs=pl.BlockSpec((1, W, SWA_HEADS * HEAD_DIM), cur),
        compiler_params=_cparams(("arbitrary", "arbitrary")),
        name="window_attn",
    )(qbp, kb, kb, kb, vb, vb, vb, vbs, vbs, vbs, sink.reshape(1, SWA_HEADS))


def _prep_w_in_odd(w):
    z = lambda n: jnp.zeros((w.shape[0], n), w.dtype)
    kr = jnp.concatenate([z(64), w[:, 384:416], z(32)], axis=1)
    return jnp.concatenate([w[:, 0:256], w[:, 256:384], kr, w[:, 416:928] * NA_DH ** -0.5,
                            w[:, 928:1440], w[:, 1440:1952]], axis=1).astype(BF16)


def _prep_mla_up(w_uq, w_ukv):
    zq = jnp.zeros((w_uq.shape[0], 32), w_uq.dtype)
    zk = jnp.zeros((w_ukv.shape[0], 64), w_ukv.dtype)
    qd = MLA_NOPE + MLA_ROPE
    kd = MLA_NOPE + MLA_VDIM
    q = [jnp.concatenate([w_uq[:, qd * h:qd * h + qd], zq], axis=1) for h in range(MLA_HEADS)]
    k = [jnp.concatenate([w_ukv[:, kd * h:kd * h + MLA_NOPE], zk], axis=1) for h in range(MLA_HEADS)]
    v = [jnp.concatenate([w_ukv[:, kd * h + MLA_NOPE:kd * h + kd], zk], axis=1) for h in range(MLA_HEADS)]
    cat = lambda xs: jnp.concatenate(xs, axis=1).astype(BF16)
    return cat(q), cat(k), cat(v)


def _inproj_odd_kernel(x_ref, sc_ref, sh_ref, w_ref, qn_ref, kvn_ref, wq_ref, wk_ref, wv_ref, cos_ref, sin_ref,
                       qm_ref, km_ref, vm_ref, nq_ref, nk_ref, nv_ref):
    h = (x_ref[0] * (1.0 + sc_ref[0]) + sh_ref[0]).astype(BF16)
    cos, sin = cos_ref[...], sin_ref[...]
    half = MLA_ROPE // 2
    c_q = _dot(h, w_ref[:, 0:256])
    c_kv = _dot(h, w_ref[:, 256:384])
    kpe = _rope128(_dot(h, w_ref[:, 384:512]), cos, sin, half)
    cqn = _rms_norm(c_q, qn_ref[...]).astype(BF16)
    ckn = _rms_norm(c_kv, kvn_ref[...]).astype(BF16)
    scale = (MLA_NOPE + MLA_ROPE) ** -0.5
    for c0 in range(0, MLA_HEADS * LANES, MXU_COLS):
        q2 = _dot(cqn, wq_ref[:, c0:c0 + MXU_COLS]) * scale
        k2 = _dot(ckn, wk_ref[:, c0:c0 + MXU_COLS])
        for j in range(MXU_COLS // LANES):
            sl = slice(c0 + j * LANES, c0 + (j + 1) * LANES)
            qm_ref[0, :, sl] = _rope128(q2[:, j * LANES:(j + 1) * LANES], cos, sin, half).astype(BF16)
            km_ref[0, :, sl] = (k2[:, j * LANES:(j + 1) * LANES] + kpe).astype(BF16)
        vm_ref[0, :, c0:c0 + MXU_COLS] = _dot(ckn, wv_ref[:, c0:c0 + MXU_COLS]).astype(BF16)
    for base, o_ref in ((512, nq_ref), (1024, nk_ref), (1536, nv_ref)):
        for c0 in range(0, NA_HEADS * NA_DH, MXU_COLS):
            o_ref[0, :, c0:c0 + MXU_COLS] = _dot(h, w_ref[:, base + c0:base + c0 + MXU_COLS]).astype(BF16)


def _inproj_odd(x, sc, sh, w, q_norm, kv_norm, wq, wk, wv, cos, sin, tile):
    B, S, _ = x.shape
    row = lambda b, i: (b, i, 0)
    vec = lambda b, i: (b, 0, 0)
    full = lambda b, i: (0, 0)
    widths = (1024, 1024, 1024, 512, 512, 512)
    return pl.pallas_call(
        _inproj_odd_kernel,
        out_shape=[jax.ShapeDtypeStruct((B, S, wd), BF16) for wd in widths],
        grid=(B, S // tile),
        in_specs=[pl.BlockSpec((1, tile, D_MODEL), row),
                  pl.BlockSpec((1, 1, D_MODEL), vec), pl.BlockSpec((1, 1, D_MODEL), vec),
                  pl.BlockSpec(w.shape, full),
                  pl.BlockSpec((1, MLA_Q_RANK), full), pl.BlockSpec((1, MLA_KV_RANK), full),
                  pl.BlockSpec(wq.shape, full), pl.BlockSpec(wk.shape, full), pl.BlockSpec(wv.shape, full),
                  pl.BlockSpec((tile, LANES), lambda b, i: (i, 0)),
                  pl.BlockSpec((tile, LANES), lambda b, i: (i, 0))],
        out_specs=[pl.BlockSpec((1, tile, wd), row) for wd in widths],
        compiler_params=_cparams(("arbitrary", "arbitrary"), 48 * 1024 * 1024),
        name="inproj_odd",
    )(x, sc, sh, w, q_norm.reshape(1, -1), kv_norm.reshape(1, -1), wq, wk, wv, cos, sin)


def _mla_attn_kernel(q_ref, k_ref, v_ref, o_ref, m_ref, l_ref, acc_ref, *, tk):
    S = k_ref.shape[1]
    q = q_ref[0]
    m_ref[...] = jnp.full(m_ref.shape, NEG_INF, F32)
    l_ref[...] = jnp.zeros(l_ref.shape, F32)
    acc_ref[...] = jnp.zeros(acc_ref.shape, F32)

    def body(j, carry):
        off = pl.multiple_of(j * tk, tk)
        k = k_ref[0, pl.ds(off, tk), :]
        v = v_ref[0, pl.ds(off, tk), :]
        s = _dot_nt(q, k)
        m_prev = m_ref[...]
        m_new = jnp.maximum(m_prev, jnp.max(s, axis=1, keepdims=True))
        p = jnp.exp(s - _lane_rep(m_new, tk))
        alpha = jnp.exp(m_prev - m_new)
        l_ref[...] = alpha * l_ref[...] + jnp.sum(p, axis=1, keepdims=True)
        acc_ref[...] = alpha * acc_ref[...] + _dot(p.astype(BF16), v)
        m_ref[...] = m_new
        return carry

    lax.fori_loop(0, S // tk, body, 0)
    o_ref[0] = (acc_ref[...] / l_ref[...]).astype(BF16)


def _mla_attention(qm, km, vm, tq, tk):
    B, S, _ = qm.shape
    return pl.pallas_call(
        functools.partial(_mla_attn_kernel, tk=tk),
        out_shape=jax.ShapeDtypeStruct((B, S, MLA_HEADS * LANES), BF16),
        grid=(B, MLA_HEADS, S // tq),
        in_specs=[pl.BlockSpec((1, tq, LANES), lambda b, h, i: (b, i, h)),
                  pl.BlockSpec((1, S, LANES), lambda b, h, i: (b, 0, h)),
                  pl.BlockSpec((1, S, LANES), lambda b, h, i: (b, 0, h))],
        out_specs=pl.BlockSpec((1, tq, LANES), lambda b, h, i: (b, i, h)),
        scratch_shapes=[pltpu.VMEM((tq, LANES), F32)] * 3,
        compiler_params=_cparams(("arbitrary",) * 3, 48 * 1024 * 1024),
        name="mla_attn",
    )(qm, km, vm)


def _na_bias_table(rpb):
    nq, nk = NA_QROWS * GRID_W, NA_KROWS * GRID_W
    qi, ki = jnp.arange(nq), jnp.arange(nk)
    qr, qc = qi // GRID_W, qi % GRID_W
    kr, kc = ki // GRID_W, ki % GRID_W
    qcs = jnp.clip(qc - NA_KC // 2, 0, GRID_W - NA_KC)
    col_ok = (kc[None, :] >= qcs[:, None]) & (kc[None, :] < qcs[:, None] + NA_KC)
    dc = jnp.clip(kc[None, :] - qc[:, None] + (NA_KC - 1), 0, 2 * NA_KC - 2)
    tabs = []
    for pat in range(3):
        rel_r = qr + 4 * pat
        w0 = jnp.clip(rel_r - NA_KR // 2, 0, NA_KROWS - NA_KR)
        row_ok = (kr[None, :] >= w0[:, None]) & (kr[None, :] < w0[:, None] + NA_KR)
        dr = jnp.clip(kr[None, :] - rel_r[:, None] + (NA_KR - 1), 0, 2 * NA_KR - 2)
        tabs.append(jnp.where((row_ok & col_ok)[None], rpb[:, dr, dc].astype(F32), NEG_INF))
    return jnp.stack(tabs)


def _na_kernel(q_ref, k_ref, v_ref, bias_ref, o_ref, *, rows):
    nq, nk = NA_QROWS * GRID_W, NA_KROWS * GRID_W
    blk = pl.program_id(2)
    k0 = jnp.clip(blk * NA_QROWS - NA_KR // 2, 0, rows - NA_KROWS) * GRID_W
    k0 = pl.multiple_of(k0, GRID_W)
    k = k_ref[0, pl.ds(k0, nk), :]
    v = v_ref[0, pl.ds(k0, nk), :]
    q = q_ref[0]
    lane = lax.broadcasted_iota(I32, q.shape, 1)
    zero = jnp.zeros_like(q)
    outs = []
    for hh in range(2):
        qh = jnp.where((lane < NA_DH) == (hh == 0), q, zero)
        s = _dot_nt(qh, k) + bias_ref[0, hh]
        m = jnp.max(s, axis=1, keepdims=True)
        e = jnp.exp(s - m)
        p = e / jnp.sum(e, axis=1, keepdims=True)
        outs.append(_dot(p.astype(BF16), v))
    o_ref[0] = jnp.where(lane < NA_DH, outs[0], outs[1]).astype(BF16)


def _na_attention(nq_, nk_, nv_, bias):
    B, S, _ = nq_.shape
    rows = S // GRID_W
    nblk = rows // NA_QROWS
    nq, nk = NA_QROWS * GRID_W, NA_KROWS * GRID_W

    def pat(b, p, i):
        return (jnp.where(i == 0, 0, jnp.where(i == nblk - 1, 2, 1)), p, 0, 0)

    return pl.pallas_call(
        functools.partial(_na_kernel, rows=rows),
        out_shape=jax.ShapeDtypeStruct((B, S, NA_HEADS * NA_DH), BF16),
        grid=(B, NA_HEADS // 2, nblk),
        in_specs=[pl.BlockSpec((1, nq, LANES), lambda b, p, i: (b, i, p)),
                  pl.BlockSpec((1, S, LANES), lambda b, p, i: (b, 0, p)),
                  pl.BlockSpec((1, S, LANES), lambda b, p, i: (b, 0, p)),
                  pl.BlockSpec((1, 2, nq, nk), pat)],
        out_specs=pl.BlockSpec((1, nq, LANES), lambda b, p, i: (b, i, p)),
        compiler_params=_cparams(("arbitrary",) * 3, 48 * 1024 * 1024),
        name="na_attn",
    )(nq_, nk_, nv_, bias)


def _outproj_kernel(oa_ref, ob_ref, wa_ref, wb_ref, x_ref, g1_ref, lng_ref, lnb_ref, sc2_ref, sh2_ref,
                    rwh_ref, rwl_ref, x1_ref, h2_ref, aff_ref):
    y = _dot(oa_ref[0], wa_ref[...]) + _dot(ob_ref[0], wb_ref[...])
    x1 = _layer_norm(DEEPNORM_ALPHA * x_ref[0] + g1_ref[0] * y, lng_ref[...], lnb_ref[...])
    x1_ref[0] = x1
    h2 = x1 * (1.0 + sc2_ref[0]) + sh2_ref[0]
    h2_ref[0] = h2
    hh = h2.astype(BF16)
    hl = (h2 - hh.astype(F32)).astype(BF16)
    rwh, rwl = rwh_ref[...], rwl_ref[...]
    lg = _dot_nt(rwh, hh) + (_dot_nt(rwl, hh) + _dot_nt(rwh, hl))
    e = jnp.exp(lg - jnp.max(lg, axis=0, keepdims=True))
    aff_ref[...] = e / jnp.sum(e, axis=0, keepdims=True)


def _outproj(oa, ob, wa, wb, x, g1, lng, lnb, sc2, sh2, rwh, rwl, tile):
    B, S, _ = x.shape
    nt = S // tile
    row = lambda b, i: (b, i, 0)
    vec = lambda b, i: (b, 0, 0)
    full = lambda b, i: (0, 0)
    return pl.pallas_call(
        _outproj_kernel,
        out_shape=[jax.ShapeDtypeStruct((B, S, D_MODEL), F32), jax.ShapeDtypeStruct((B, S, D_MODEL), F32),
                   jax.ShapeDtypeStruct((N_EXPERTS, B * S), F32)],
        grid=(B, nt),
        in_specs=[pl.BlockSpec((1, tile, oa.shape[2]), row), pl.BlockSpec((1, tile, ob.shape[2]), row),
                  pl.BlockSpec(wa.shape, full), pl.BlockSpec(wb.shape, full),
                  pl.BlockSpec((1, tile, D_MODEL), row), pl.BlockSpec((1, 1, D_MODEL), vec),
                  pl.BlockSpec((1, D_MODEL), full), pl.BlockSpec((1, D_MODEL), full),
                  pl.BlockSpec((1, 1, D_MODEL), vec), pl.BlockSpec((1, 1, D_MODEL), vec),
                  pl.BlockSpec((N_EXPERTS, D_MODEL), full), pl.BlockSpec((N_EXPERTS, D_MODEL), full)],
        out_specs=[pl.BlockSpec((1, tile, D_MODEL), row), pl.BlockSpec((1, tile, D_MODEL), row),
                   pl.BlockSpec((N_EXPERTS, tile), lambda b, i: (0, b * nt + i))],
        compiler_params=_cparams(("arbitrary", "arbitrary"), 48 * 1024 * 1024),
        name="outproj_ln_router",
    )(oa, ob, wa, wb, x, g1, lng.reshape(1, -1), lnb.reshape(1, -1), sc2, sh2, rwh, rwl)


def _select_kernel(aff_ref, sel_ref, pos_ref, off_ref, *, cap):
    E, NT, _ = aff_ref.shape
    keys = lax.bitcast_convert_type(aff_ref[...], I32)

    def count_ge(c):
        hit = jnp.where(keys >= c, 1.0, 0.0)
        return jnp.sum(jnp.sum(hit, axis=2, keepdims=True), axis=1, keepdims=True)

    def bit_body(t, res):
        cand = res | (jnp.int32(1) << (30 - t))
        return jnp.where(count_ge(cand) >= cap, cand, res)

    thr = lax.fori_loop(0, 31, bit_body, jnp.zeros((E, 1, 1), I32))

    upper = jnp.where(lax.broadcasted_iota(I32, (LANES, LANES), 0) < lax.broadcasted_iota(I32, (LANES, LANES), 1),
                      1.0, 0.0).astype(BF16)
    lower = jnp.where(lax.broadcasted_iota(I32, (NT, NT), 1) < lax.broadcasted_iota(I32, (NT, NT), 0),
                      1.0, 0.0).astype(BF16)

    def excl_prefix(m):
        loc = _dot(m.astype(BF16), upper)
        cnt = jnp.broadcast_to(jnp.sum(m, axis=1, keepdims=True), m.shape).astype(BF16)
        start = _dot(lower, cnt)
        return start + loc, start

    for e in range(E):
        ke, te = keys[e], thr[e]
        gt = jnp.where(ke > te, 1.0, 0.0)
        eq = jnp.where(ke == te, 1.0, 0.0)
        need = cap - jnp.sum(jnp.sum(gt, axis=1, keepdims=True), axis=0, keepdims=True)
        tie_rank, _ = excl_prefix(eq)
        sel = jnp.maximum(gt, jnp.where(tie_rank < need, eq, 0.0))
        pos, start = excl_prefix(sel)
        sel_ref[e] = sel
        pos_ref[e] = pos
        off_ref[e] = start


def _select(aff3, cap):
    shp = jax.ShapeDtypeStruct(aff3.shape, F32)
    return pl.pallas_call(
        functools.partial(_select_kernel, cap=cap),
        out_shape=[shp, shp, shp],
        compiler_params=_cparams(None, 48 * 1024 * 1024),
        name="ec_select",
    )(aff3)


def _compact_kernel(offs_ref, aff_ref, sel_ref, pos_ref, flat_ref, *, tb, nt):
    i = pl.program_id(0)

    @pl.when(i == 0)
    def _():
        flat_ref[...] = jnp.zeros(flat_ref.shape, F32)

    sub = lax.broadcasted_iota(I32, (2 * LANES, LANES), 0).astype(F32)
    row = lax.broadcasted_iota(I32, (8, LANES), 0)
    lane = lax.broadcasted_iota(I32, (1, LANES), 1)
    for j in range(tb):
        tile = i * tb + j
        tok = tile * LANES + lane
        t_hi = (tok >> 8).astype(F32)
        t_lo = (tok & 255).astype(F32)
        for e in range(N_EXPERTS):
            q = offs_ref[e * nt + tile] >> 7
            g = aff_ref[e, j:j + 1, :]
            g_hi = g.astype(BF16).astype(F32)
            g_mid = (g - g_hi).astype(BF16).astype(F32)
            g_lo = g - g_hi - g_mid
            vals = jnp.where(row == 0, t_hi, jnp.where(row == 1, t_lo, jnp.where(
                row == 2, g_hi, jnp.where(row == 3, g_mid, jnp.where(row == 4, g_lo, 0.0)))))
            prel = jnp.where(sel_ref[e, j:j + 1, :] > 0.0,
                             pos_ref[e, j:j + 1, :] - (q * LANES).astype(F32), -1.0)
            onehot = jnp.where(sub == prel, 1.0, 0.0).astype(BF16)
            res = _dot_nt(vals.astype(BF16), onehot)
            flat_ref[e, q] += res[:, :LANES]
            flat_ref[e, q + 1] += res[:, LANES:]


def _compact(offs, aff3, sel3, pos3, cap, tb):
    E, NT, _ = aff3.shape
    nq = cap // LANES
    blk = pl.BlockSpec((E, tb, LANES), lambda i, offs: (0, i, 0))
    return pl.pallas_call(
        functools.partial(_compact_kernel, tb=tb, nt=NT),
        out_shape=jax.ShapeDtypeStruct((E, nq + 2, 8, LANES), F32),
        grid_spec=pltpu.PrefetchScalarGridSpec(
            num_scalar_prefetch=1, grid=(NT // tb,),
            in_specs=[blk, blk, blk],
            out_specs=pl.BlockSpec((E, nq + 2, 8, LANES), lambda i, offs: (0, 0, 0, 0))),
        compiler_params=_cparams(("arbitrary",)),
        name="ec_compact",
    )(offs, aff3, sel3, pos3)


def _ffn_kernel(idx_ref, h2_hbm, gate_ref, wg_ref, wu_ref, wd_ref, y_ref, xbuf, hbuf, sem, *, m, fb):
    nsteps = pl.num_programs(0) * pl.num_programs(1)
    step = pl.program_id(0) * pl.num_programs(1) + pl.program_id(1)
    slot = step % 2

    def row_copy(tok, sl, r):
        return pltpu.make_async_copy(h2_hbm.at[pl.ds(tok, 1), :], xbuf.at[sl, pl.ds(r, 1), :], sem.at[sl])

    def issue(st, sl):
        def body(r, c):
            row_copy(idx_ref[st * m + r], sl, r).start()
            return c
        lax.fori_loop(0, m, body, 0)

    @pl.when(step == 0)
    def _():
        issue(0, 0)

    @pl.when(step + 1 < nsteps)
    def _():
        issue(step + 1, 1 - slot)

    def wait_body(r, c):
        row_copy(0, slot, r).wait()
        return c
    lax.fori_loop(0, m, wait_body, 0)

    x = xbuf[slot].astype(BF16)
    nf = wg_ref.shape[2] // fb
    for f in range(nf):
        a = _dot(x, wg_ref[0, :, f * fb:(f + 1) * fb])
        u = _dot(x, wu_ref[0, :, f * fb:(f + 1) * fb])
        hbuf[:, f * fb:(f + 1) * fb] = (a * jax.nn.sigmoid(a) * u).astype(BF16)
    y = _dot(hbuf[...], wd_ref[0])
    eye = lax.broadcasted_iota(I32, (LANES, LANES), 0) == lax.broadcasted_iota(I32, (LANES, LANES), 1)
    cols = [jnp.sum(jnp.where(eye, gate_ref[0, r:r + 1, :], 0.0), axis=1, keepdims=True)
            for r in range(m // LANES)]
    y_ref[...] = (y * jnp.concatenate(cols, axis=0)).astype(BF16)


def _expert_ffn(idx, h2, gates, wg, wu, wd, cap, m):
    E = N_EXPERTS
    nc = cap // m
    f = wg.shape[2]
    return pl.pallas_call(
        functools.partial(_ffn_kernel, m=m, fb=256),
        out_shape=jax.ShapeDtypeStruct((E * cap, D_MODEL), BF16),
        grid_spec=pltpu.PrefetchScalarGridSpec(
            num_scalar_prefetch=1, grid=(E, nc),
            in_specs=[pl.BlockSpec(memory_space=pl.ANY),
                      pl.BlockSpec((1, m // LANES, LANES), lambda e, c, idx: (e * nc + c, 0, 0)),
                      pl.BlockSpec((1, D_MODEL, f), lambda e, c, idx: (e, 0, 0)),
                      pl.BlockSpec((1, D_MODEL, f), lambda e, c, idx: (e, 0, 0)),
                      pl.BlockSpec((1, f, D_MODEL), lambda e, c, idx: (e, 0, 0))],
            out_specs=pl.BlockSpec((m, D_MODEL), lambda e, c, idx: (e * nc + c, 0)),
            scratch_shapes=[pltpu.VMEM((2, m, D_MODEL), F32), pltpu.VMEM((m, f), BF16),
                            pltpu.SemaphoreType.DMA((2,))]),
        compiler_params=_cparams(("arbitrary", "arbitrary"), VMEM_LIMIT_V7X),
        name="expert_ffn",
    )(idx, h2, gates.reshape(E * nc, m // LANES, LANES), wg, wu, wd)


COMBINE_WIN = 256


def _combine_kernel(offc_ref, x1_ref, g2_ref, lng_ref, lnb_ref, idx_ref, y_hbm, o_ref, ybuf, sem, acc_ref,
                    *, cap, ntc, tc):
    E, W = N_EXPERTS, COMBINE_WIN
    i = pl.program_id(0)

    def window_copy(e, wc):
        return pltpu.make_async_copy(y_hbm.at[pl.ds(e * cap + wc, W), :], ybuf.at[pl.ds(e * W, W), :], sem.at[e])

    def first_window(e):
        return (offc_ref[e * (ntc + 1) + i] >> 7) * LANES

    def clamp(wu):
        return pl.multiple_of(jnp.minimum(wu, cap - W), LANES)

    for e in range(E):
        window_copy(e, clamp(first_window(e))).start()

    tok = i * tc + lax.broadcasted_iota(I32, (tc, LANES), 0)
    lane = lax.broadcasted_iota(I32, (1, LANES), 1)

    def onehot(e, wu):
        wc = clamp(wu)
        halves = []
        for hf in range(W // LANES):
            ids = idx_ref[e, pl.ds((wc >> 7) + hf, 1), :]
            ids = jnp.where(wc + hf * LANES + lane >= wu, ids, -1)
            halves.append(jnp.where(tok == ids, 1.0, 0.0).astype(BF16))
        return jnp.concatenate(halves, axis=1)

    ps = []
    for e in range(E):
        ps.append(onehot(e, first_window(e)))
    for e in range(E):
        window_copy(e, 0).wait()
    acc_ref[...] = _dot(jnp.concatenate(ps, axis=1), ybuf[...])

    for e in range(E):
        wu0 = first_window(e)
        left = offc_ref[e * (ntc + 1) + i + 1] - (wu0 + W)
        nextra = jnp.where(left > 0, (left + W - 1) >> 8, 0)

        def extra(kk, c, e=e, wu0=wu0):
            wu = wu0 + (kk + 1) * W
            cp = window_copy(e, clamp(wu))
            cp.start()
            cp.wait()
            acc_ref[...] += _dot(onehot(e, wu), ybuf[e * W:(e + 1) * W, :])
            return c
        lax.fori_loop(0, nextra, extra, 0)

    z = DEEPNORM_ALPHA * x1_ref[...] + g2_ref[0] * acc_ref[...]
    o_ref[...] = _layer_norm(z, lng_ref[...], lnb_ref[...])


def _combine(offc, x1, g2, lng, lnb, idx3, y, cap, tc, tiles_per_seq):
    n = x1.shape[0]
    ntc = n // tc
    E, W = N_EXPERTS, COMBINE_WIN
    return pl.pallas_call(
        functools.partial(_combine_kernel, cap=cap, ntc=ntc, tc=tc),
        out_shape=jax.ShapeDtypeStruct((n, D_MODEL), F32),
        grid_spec=pltpu.PrefetchScalarGridSpec(
            num_scalar_prefetch=1, grid=(ntc,),
            in_specs=[pl.BlockSpec((tc, D_MODEL), lambda i, o: (i, 0)),
                      pl.BlockSpec((1, 1, D_MODEL), lambda i, o: (i // tiles_per_seq, 0, 0)),
                      pl.BlockSpec((1, D_MODEL), lambda i, o: (0, 0)),
                      pl.BlockSpec((1, D_MODEL), lambda i, o: (0, 0)),
                      pl.BlockSpec(idx3.shape, lambda i, o: (0, 0, 0)),
                      pl.BlockSpec(memory_space=pl.ANY)],
            out_specs=pl.BlockSpec((tc, D_MODEL), lambda i, o: (i, 0)),
            scratch_shapes=[pltpu.VMEM((E * W, D_MODEL), BF16), pltpu.SemaphoreType.DMA((E,)),
                            pltpu.VMEM((tc, D_MODEL), F32)]),
        compiler_params=_cparams(("arbitrary",), 48 * 1024 * 1024),
        name="ec_combine_ln",
    )(offc, x1, g2, lng.reshape(1, -1), lnb.reshape(1, -1), idx3, y)


def _rope_tables(seq, dim, lo, period):
    inv = 1.0 / (ROPE_THETA ** (jnp.arange(0, dim, 2, dtype=F32) / dim))
    ang = jnp.arange(seq, dtype=F32)[:, None] * inv[None, :]
    lane = jnp.arange(LANES)
    rel = (lane - lo) % period
    on = (lane >= lo) & (rel < dim)
    j = rel % (dim // 2)
    cos = jnp.where(on[None, :], jnp.cos(ang)[:, j], 1.0)
    sgn = jnp.where(rel < dim // 2, -1.0, 1.0)
    sin = jnp.where(on[None, :], jnp.sin(ang)[:, j] * sgn[None, :], 0.0)
    return cos, sin


def _split_bf16(w):
    hi = w.astype(BF16)
    return hi, (w - hi.astype(F32)).astype(BF16)


def _moe(x1, h2, affT, g2, lng, lnb, wg, wu, wd, B, S):
    n = B * S
    E = N_EXPERTS
    cap = EC_CAPACITY_FACTOR * n // E
    nt = n // LANES
    aff3 = affT.reshape(E, nt, LANES)
    sel3, pos3, off3 = _select(aff3, cap)
    offs = off3[:, :, 0].astype(I32)
    flat = _compact(offs.reshape(-1), aff3, sel3, pos3, cap, min(8, nt))
    nq = cap // LANES
    idx3 = (flat[:, :nq, 0, :] * 256.0 + flat[:, :nq, 1, :]).astype(I32)
    gates = flat[:, :nq, 2, :] + flat[:, :nq, 3, :] + flat[:, :nq, 4, :]
    m = min(512, cap)
    y = _expert_ffn(idx3.reshape(-1), h2.reshape(n, D_MODEL), gates, wg, wu, wd, cap, m)
    tc = 512
    stride = tc // LANES
    offc = jnp.concatenate([offs[:, ::stride], jnp.full((E, 1), cap, I32)], axis=1)
    return _combine(offc.reshape(-1), x1.reshape(n, D_MODEL), g2, lng, lnb, idx3, y, cap, tc, S // tc)


def _trunk(x, mod, W):
    B, S, _ = x.shape
    tile = 512
    tq = min(512, S)
    tk = min(512, S)
    cos_h, sin_h = _rope_tables(S, HEAD_DIM, 0, HEAD_DIM)
    cos_r, sin_r = _rope_tables(S, MLA_ROPE, MLA_NOPE, LANES)
    for l in range(DEPTH):
        sh1, sc1, g1, sh2, sc2, g2 = [mod[l][:, None, k * D_MODEL:(k + 1) * D_MODEL] for k in range(6)]
        if l % 2 == 0:
            e = l // 2
            lam_init = 0.8 - 0.6 * math.exp(-0.3 * l)
            qa, ka, va, qbp, kb, vb, vbs = _inproj_even(x, sc1, sh1, W["w_in_even"][e], cos_h, sin_h, tile)
            oa = _diff_attention(qa, ka, va, W["diff_lambda"][e], W["diff_subln"][e], lam_init, tq, tk)
            ob = _window_attention(qbp, kb, vb, vbs, W["swa_sink"][e])
            wa, wb = W["w_out_even"][e]
        else:
            o = l // 2
            qm, km, vm, nq_, nk_, nv_ = _inproj_odd(x, sc1, sh1, W["w_in_odd"][o], W["mla_q_norm"][o],
                                                    W["mla_kv_norm"][o], *W["mla_up"][o], cos_r, sin_r, tile)
            oa = _mla_attention(qm, km, vm, tq, tk)
            ob = _na_attention(nq_, nk_, nv_, W["na_bias"][o])
            wa, wb = W["w_out_odd"][o]
        x1, h2, affT = _outproj(oa, ob, wa, wb, x, g1, W["ln_g"][l, 0], W["ln_b"][l, 0], sc2, sh2,
                                *W["router"][l], tile)
        x = _moe(x1, h2, affT, g2, W["ln_g"][l, 1], W["ln_b"][l, 1],
                 W["exp_w_gate"][l], W["exp_w_up"][l], W["exp_w_down"][l], B, S).reshape(B, S, D_MODEL)
    return x


def kernel(x_prompt, x_sample, c_prompt, c_sample, w_in_even, diff_lambda, diff_subln, swa_sink, w_out_even,
           w_in_odd, mla_q_norm, mla_w_uq, mla_kv_norm, mla_w_ukv, na_rpb, w_out_odd, ada_w, ada_b, ln_g, ln_b,
           router_w, exp_w_gate, exp_w_up, exp_w_down):
    bp, bs = x_prompt.shape[0], x_sample.shape[0]
    c_all = jnp.concatenate([c_prompt, c_sample, jnp.zeros((8 - bp - bs, D_MODEL), F32)], axis=0)
    mod = _modulation(c_all, ada_w, ada_b)
    n_even, n_odd = w_in_even.shape[0], w_in_odd.shape[0]
    ka = DIFF_HEADS * 2 * DIFF_DH
    zpad = jnp.zeros((MLA_HEADS, LANES - MLA_VDIM, D_MODEL), F32)

    def pad_mla_rows(w):
        return jnp.concatenate([w.reshape(MLA_HEADS, MLA_VDIM, D_MODEL), zpad], axis=1).reshape(-1, D_MODEL)

    km = MLA_HEADS * MLA_VDIM
    W = {
        "w_in_even": [_prep_w_in_even(w_in_even[e]) for e in range(n_even)],
        "diff_lambda": diff_lambda, "diff_subln": diff_subln, "swa_sink": swa_sink,
        "w_out_even": [(w_out_even[e, :ka].astype(BF16), w_out_even[e, ka:].astype(BF16)) for e in range(n_even)],
        "w_in_odd": [_prep_w_in_odd(w_in_odd[o]) for o in range(n_odd)],
        "mla_q_norm": mla_q_norm, "mla_kv_norm": mla_kv_norm,
        "mla_up": [_prep_mla_up(mla_w_uq[o], mla_w_ukv[o]) for o in range(n_odd)],
        "na_bias": [_na_bias_table(na_rpb[o]) for o in range(n_odd)],
        "w_out_odd": [(pad_mla_rows(w_out_odd[o, :km]).astype(BF16), w_out_odd[o, km:].astype(BF16))
                      for o in range(n_odd)],
        "ln_g": ln_g, "ln_b": ln_b,
        "router": [_split_bf16(router_w[l].T) for l in range(DEPTH)],
        "exp_w_gate": exp_w_gate.astype(BF16), "exp_w_up": exp_w_up.astype(BF16),
        "exp_w_down": exp_w_down.astype(BF16),
    }
    y_prompt = _trunk(x_prompt, mod[:, 0:bp], W)
    y_sample = _trunk(x_sample, mod[:, bp:bp + bs], W)
    return (y_prompt, y_sample)
```

```python
import functools
import math

import jax
import jax.numpy as jnp
import numpy as np
from jax import lax
from jax.experimental import pallas as pl
from jax.experimental.pallas import tpu as pltpu

F32, BF16, I32 = jnp.float32, jnp.bfloat16, jnp.int32

D_MODEL = 1024
DEPTH = 2
DIFF_HEADS, DIFF_DH = 4, 64
SWA_HEADS, SWA_KV_HEADS, SWA_WINDOW, HEAD_DIM = 8, 2, 128, 64
MLA_HEADS, MLA_Q_RANK, MLA_KV_RANK, MLA_NOPE, MLA_ROPE, MLA_VDIM = 8, 256, 128, 64, 32, 64
NA_HEADS, NA_DH, NA_KR, NA_KC, GRID_W = 8, 64, 8, 16, 64
N_EXPERTS, EC_CAPACITY_FACTOR, D_EXPERT = 16, 2, 2816
ROPE_THETA = 10000.0
DEEPNORM_ALPHA = (2 * DEPTH) ** 0.25
LN_EPS, RMS_EPS, NEG_INF = 1e-5, 1e-6, -1e30
LOG2E = 1.4426950408889634

LANES = 128
MXU_COLS = 256
VMEM_LIMIT_V7X = 56 * 1024 * 1024
NA_QROWS, NA_KROWS = 8, 16
NT_DIMS = (((1,), (1,)), ((), ()))


def _cparams(sem, vmem=None):
    return pltpu.CompilerParams(dimension_semantics=sem, vmem_limit_bytes=vmem)


def _dot(a, b):
    return jnp.dot(a, b, preferred_element_type=F32)


def _dot_nt(a, b):
    return lax.dot_general(a, b, NT_DIMS, preferred_element_type=F32)


def _layer_norm(z, g, b):
    mu = jnp.mean(z, axis=-1, keepdims=True)
    zc = z - mu
    var = jnp.mean(zc * zc, axis=-1, keepdims=True)
    return zc * lax.rsqrt(var + LN_EPS) * g + b


def _rms_norm(x, g):
    return x * lax.rsqrt(jnp.mean(x * x, axis=-1, keepdims=True) + RMS_EPS) * g


def _rope128(r, cos, sin, half):
    lane = lax.broadcasted_iota(I32, r.shape, 1)
    first = (lane % (2 * half)) < half
    rot = jnp.where(first, pltpu.roll(r, LANES - half, 1), pltpu.roll(r, half, 1))
    return r * cos + rot * sin


def _lane_rep(m, width):
    return m if width == LANES else jnp.concatenate([m] * (width // LANES), axis=1)


def _mod_kernel(c_ref, w_ref, b_ref, o_ref):
    c = c_ref[...]
    a = (c * jax.nn.sigmoid(c)).astype(BF16)
    o_ref[0] = _dot(a, w_ref[0].astype(BF16)) + b_ref[0]


def _modulation(c_all, ada_w, ada_b):
    nb = 1536
    n6 = ada_w.shape[2]
    return pl.pallas_call(
        _mod_kernel,
        out_shape=jax.ShapeDtypeStruct((DEPTH, 8, n6), F32),
        grid=(DEPTH, n6 // nb),
        in_specs=[pl.BlockSpec((8, D_MODEL), lambda l, j: (0, 0)),
                  pl.BlockSpec((1, D_MODEL, nb), lambda l, j: (l, 0, j)),
                  pl.BlockSpec((1, 1, nb), lambda l, j: (l, 0, j))],
        out_specs=pl.BlockSpec((1, 8, nb), lambda l, j: (l, 0, j)),
        compiler_params=_cparams(("arbitrary", "arbitrary"), 40 * 1024 * 1024),
        name="adaln_mod",
    )(c_all, ada_w, ada_b.reshape(DEPTH, 1, n6))


EVEN_SEGS = (("qa", 512, True, LOG2E, False), ("ka", 512, True, None, False), ("va", 1024, False, None, True),
             ("qb", 1024, True, None, False), ("kb", 128, True, None, False), ("vb", 128, False, None, False),
             ("vbs", 128, False, None, False))


def _prep_w_in_even(w):
    scale = HEAD_DIM ** -0.5
    qa, ka, va = w[:, 0:512] * scale, w[:, 512:1024], w[:, 1024:1536]
    qb, kb, vb = w[:, 1536:2048] * scale, w[:, 2048:2176], w[:, 2176:2304]
    rep = SWA_HEADS // SWA_KV_HEADS
    z = jnp.zeros((w.shape[0], 64), w.dtype)
    qbp = []
    for h in range(SWA_HEADS):
        wh = qb[:, 64 * h:64 * h + 64]
        qbp += [wh, z] if h // rep == 0 else [z, wh]
    vbs = jnp.concatenate([vb[:, 64:], vb[:, :64]], axis=1)
    z128 = jnp.zeros((w.shape[0], LANES), w.dtype)
    vap = []
    for h in range(DIFF_HEADS):
        vap += [va[:, LANES * h:LANES * (h + 1)], z128]
    return jnp.concatenate([qa, ka] + vap + qbp + [kb, vb, vbs], axis=1).astype(BF16)


def _inproj_even_kernel(x_ref, sc_ref, sh_ref, w_ref, cos_ref, sin_ref, *out_refs):
    h = (x_ref[0] * (1.0 + sc_ref[0]) + sh_ref[0]).astype(BF16)
    cos, sin = cos_ref[...], sin_ref[...]
    off = 0
    for (_, width, rope, mul, ones), o_ref in zip(EVEN_SEGS, out_refs):
        for c0 in range(0, width, MXU_COLS):
            cw = min(MXU_COLS, width - c0)
            r = _dot(h, w_ref[:, off + c0:off + c0 + cw])
            if mul is not None:
                r = r * mul
            if ones:
                r = jnp.where(lax.broadcasted_iota(I32, r.shape, 1) == LANES, 1.0, r)
            for j in range(cw // LANES):
                rj = r[:, j * LANES:(j + 1) * LANES]
                if rope:
                    rj = _rope128(rj, cos, sin, HEAD_DIM // 2)
                o_ref[0, :, c0 + j * LANES:c0 + (j + 1) * LANES] = rj.astype(BF16)
        off += width


def _inproj_even(x, sc, sh, w, cos, sin, tile):
    B, S, _ = x.shape
    ntot = w.shape[1]
    row = lambda b, i: (b, i, 0)
    vec = lambda b, i: (b, 0, 0)
    return pl.pallas_call(
        _inproj_even_kernel,
        out_shape=[jax.ShapeDtypeStruct((B, S, seg[1]), BF16) for seg in EVEN_SEGS],
        grid=(B, S // tile),
        in_specs=[pl.BlockSpec((1, tile, D_MODEL), row),
                  pl.BlockSpec((1, 1, D_MODEL), vec), pl.BlockSpec((1, 1, D_MODEL), vec),
                  pl.BlockSpec((D_MODEL, ntot), lambda b, i: (0, 0)),
                  pl.BlockSpec((tile, LANES), lambda b, i: (i, 0)),
                  pl.BlockSpec((tile, LANES), lambda b, i: (i, 0))],
        out_specs=[pl.BlockSpec((1, tile, seg[1]), row) for seg in EVEN_SEGS],
        compiler_params=_cparams(("arbitrary", "arbitrary"), 48 * 1024 * 1024),
        name="inproj_even",
    )(x, sc, sh, w, cos, sin)


def _flash_sweep(qs, k_ref, v_ref, m_ref, acc_ref, tk, unroll):
    S = k_ref.shape[1]
    accw = acc_ref.shape[2]
    m_ref[...] = jnp.full(m_ref.shape, NEG_INF, F32)
    acc_ref[...] = jnp.zeros(acc_ref.shape, F32)

    def body(j, carry):
        off = pl.multiple_of(j * tk, tk)
        k = k_ref[0, pl.ds(off, tk), :]
        v = v_ref[0, pl.ds(off, tk), :]
        for c, q in enumerate(qs):
            s = _dot_nt(q, k)
            m_prev = m_ref[c]
            m_new = jnp.maximum(m_prev, jnp.max(s, axis=1, keepdims=True))
            p = jnp.exp2(s - _lane_rep(m_new, tk))
            alpha = jnp.exp2(m_prev - m_new)
            acc_ref[c] = _lane_rep(alpha, accw) * acc_ref[c] + _dot(p.astype(BF16), v)
            m_ref[c] = m_new
        return carry

    lax.fori_loop(0, S // tk, body, 0, unroll=unroll)


def _diff_attn_kernel(q_ref, k_ref, v_ref, lam_ref, subln_ref, o_ref, m_ref, acc_ref, *, tk, unroll, lam_init):
    q = q_ref[0]
    lane = lax.broadcasted_iota(I32, q.shape, 1)
    zero = jnp.zeros_like(q)
    qs = [jnp.where(lane < DIFF_DH, q, zero), jnp.where(lane >= DIFF_DH, q, zero)]
    _flash_sweep(qs, k_ref, v_ref, m_ref, acc_ref, tk, unroll)
    lf = lam_ref[...]
    lam = (jnp.exp(jnp.sum(lf[0:1] * lf[1:2], axis=1, keepdims=True))
           - jnp.exp(jnp.sum(lf[2:3] * lf[3:4], axis=1, keepdims=True)) + lam_init)
    a0, a1 = acc_ref[0], acc_ref[1]
    o = a0[:, :LANES] / a0[:, LANES:LANES + 1] - lam * (a1[:, :LANES] / a1[:, LANES:LANES + 1])
    o_ref[0] = (_rms_norm(o, subln_ref[...]) * (1.0 - lam_init)).astype(BF16)


def _diff_attention(qa, ka, va1, lam, subln, lam_init, tq, tk, unroll):
    B, S, _ = qa.shape
    kern = functools.partial(_diff_attn_kernel, tk=tk, unroll=unroll, lam_init=lam_init)
    return pl.pallas_call(
        kern,
        out_shape=jax.ShapeDtypeStruct((B, S, DIFF_HEADS * LANES), BF16),
        grid=(B, DIFF_HEADS, S // tq),
        in_specs=[pl.BlockSpec((1, tq, LANES), lambda b, h, i: (b, i, h)),
                  pl.BlockSpec((1, S, LANES), lambda b, h, i: (b, 0, h)),
                  pl.BlockSpec((1, S, 2 * LANES), lambda b, h, i: (b, 0, h)),
                  pl.BlockSpec((4, DIFF_DH), lambda b, h, i: (0, 0)),
                  pl.BlockSpec((1, LANES), lambda b, h, i: (0, 0))],
        out_specs=pl.BlockSpec((1, tq, LANES), lambda b, h, i: (b, i, h)),
        scratch_shapes=[pltpu.VMEM((2, tq, LANES), F32), pltpu.VMEM((2, tq, 2 * LANES), F32)],
        compiler_params=_cparams(("arbitrary",) * 3, 48 * 1024 * 1024),
        name="diff_attn",
    )(qa, ka, va1, lam, subln.reshape(1, LANES))


def _window_kernel(q_ref, kp_ref, kc_ref, kn_ref, vp_ref, vc_ref, vn_ref, wp_ref, wc_ref, wn_ref,
                   sink_ref, o_ref, *, S):
    W = SWA_WINDOW
    i = pl.program_id(1)
    k = jnp.concatenate([kp_ref[0], kc_ref[0], kn_ref[0]], axis=0)
    v = jnp.concatenate([vp_ref[0], vc_ref[0], vn_ref[0]], axis=0)
    vs = jnp.concatenate([wp_ref[0], wc_ref[0], wn_ref[0]], axis=0)
    qpos = i * W + lax.broadcasted_iota(I32, (W, 3 * W), 0)
    kpos = i * W - W + lax.broadcasted_iota(I32, (W, 3 * W), 1)
    valid = (kpos >= 0) & (kpos < S) & (jnp.abs(qpos - kpos) <= SWA_WINDOW)
    bias = jnp.where(valid, 0.0, NEG_INF)
    H = SWA_HEADS
    rep = H // SWA_KV_HEADS
    q_all = jnp.concatenate([q_ref[0, :, h * LANES:(h + 1) * LANES] for h in range(H)], axis=0)
    s = _dot_nt(q_all, k) + jnp.concatenate([bias] * H, axis=0)
    head = lax.broadcasted_iota(I32, (H * W, 1), 0) // W
    sk = jnp.zeros((H * W, 1), F32)
    for h in range(H):
        sk = jnp.where(head == h, sink_ref[0:1, h:h + 1], sk)
    m = jnp.maximum(jnp.max(s, axis=1, keepdims=True), sk)
    e = jnp.exp(s - m)
    p = (e / (jnp.sum(e, axis=1, keepdims=True) + jnp.exp(sk - m))).astype(BF16)
    o_v, o_vs = _dot(p, v), _dot(p, vs)
    lane = lax.broadcasted_iota(I32, (W, LANES), 1)
    for pair in range(H // 2):
        outs = []
        for hh in range(2):
            h = 2 * pair + hh
            src = o_v if h // rep == hh else o_vs
            outs.append(src[h * W:(h + 1) * W])
        o_ref[0, :, pair * LANES:(pair + 1) * LANES] = jnp.where(lane < HEAD_DIM, outs[0], outs[1]).astype(BF16)


def _window_attention(qbp, kb, vb, vbs, sink):
    B, S, _ = qbp.shape
    W = SWA_WINDOW
    nb = S // W
    prev = lambda b, i: (b, jnp.maximum(i - 1, 0), 0)
    cur = lambda b, i: (b, i, 0)
    nxt = lambda b, i: (b, jnp.minimum(i + 1, nb - 1), 0)
    kv_specs = [pl.BlockSpec((1, W, LANES), f) for f in (prev, cur, nxt)]
    return pl.pallas_call(
        functools.partial(_window_kernel, S=S),
        out_shape=jax.ShapeDtypeStruct((B, S, SWA_HEADS * HEAD_DIM), BF16),
        grid=(B, nb),
        in_specs=[pl.BlockSpec((1, W, SWA_HEADS * LANES), cur)] + kv_specs * 3
                 + [pl.BlockSpec((1, SWA_HEADS), lambda b, i: (0, 0))],
        out_specs=pl.BlockSpec((1, W, SWA_HEADS * HEAD_DIM), cur),
        compiler_params=_cparams(("arbitrary", "arbitrary")),
        name="window_attn",
    )(qbp, kb, kb, kb, vb, vb, vb, vbs, vbs, vbs, sink.reshape(1, SWA_HEADS))


def _prep_w_in_odd(w):
    z = lambda n: jnp.zeros((w.shape[0], n), w.dtype)
    kr = jnp.concatenate([z(64), w[:, 384:416], z(32)], axis=1)
    return jnp.concatenate([w[:, 0:256], w[:, 256:384], kr, w[:, 416:928] * NA_DH ** -0.5,
                            w[:, 928:1440], w[:, 1440:1952]], axis=1).astype(BF16)


def _prep_mla_up(w_uq, w_ukv):
    zq = jnp.zeros((w_uq.shape[0], 32), w_uq.dtype)
    zk = jnp.zeros((w_ukv.shape[0], 64), w_ukv.dtype)
    qd = MLA_NOPE + MLA_ROPE
    kd = MLA_NOPE + MLA_VDIM
    q = [jnp.concatenate([w_uq[:, qd * h:qd * h + qd], zq], axis=1) for h in range(MLA_HEADS)]
    k = [jnp.concatenate([w_ukv[:, kd * h:kd * h + MLA_NOPE], zk], axis=1) for h in range(MLA_HEADS)]
    v = [jnp.concatenate([w_ukv[:, kd * h + MLA_NOPE:kd * h + kd], zk], axis=1) for h in range(MLA_HEADS)]
    cat = lambda xs: jnp.concatenate(xs, axis=1).astype(BF16)
    return cat(q), cat(k), cat(v)


def _inproj_odd_kernel(x_ref, sc_ref, sh_ref, w_ref, qn_ref, kvn_ref, wq_ref, wk_ref, wv_ref, cos_ref, sin_ref,
                       qm_ref, km_ref, vm_ref, nq_ref, nk_ref, nv_ref):
    h = (x_ref[0] * (1.0 + sc_ref[0]) + sh_ref[0]).astype(BF16)
    cos, sin = cos_ref[...], sin_ref[...]
    half = MLA_ROPE // 2
    c_q = _dot(h, w_ref[:, 0:256])
    c_kv = _dot(h, w_ref[:, 256:384])
    kpe = _rope128(_dot(h, w_ref[:, 384:512]), cos, sin, half)
    cqn = _rms_norm(c_q, qn_ref[...]).astype(BF16)
    ckn = _rms_norm(c_kv, kvn_ref[...]).astype(BF16)
    scale = (MLA_NOPE + MLA_ROPE) ** -0.5 * LOG2E
    ones_lane = lax.broadcasted_iota(I32, (h.shape[0], MXU_COLS), 1) % LANES == MLA_VDIM
    for c0 in range(0, MLA_HEADS * LANES, MXU_COLS):
        q2 = _dot(cqn, wq_ref[:, c0:c0 + MXU_COLS]) * scale
        k2 = _dot(ckn, wk_ref[:, c0:c0 + MXU_COLS])
        for j in range(MXU_COLS // LANES):
            sl = slice(c0 + j * LANES, c0 + (j + 1) * LANES)
            qm_ref[0, :, sl] = _rope128(q2[:, j * LANES:(j + 1) * LANES], cos, sin, half).astype(BF16)
            km_ref[0, :, sl] = (k2[:, j * LANES:(j + 1) * LANES] + kpe).astype(BF16)
        v2 = jnp.where(ones_lane, 1.0, _dot(ckn, wv_ref[:, c0:c0 + MXU_COLS]))
        vm_ref[0, :, c0:c0 + MXU_COLS] = v2.astype(BF16)
    for base, o_ref in ((512, nq_ref), (1024, nk_ref), (1536, nv_ref)):
        for c0 in range(0, NA_HEADS * NA_DH, MXU_COLS):
            o_ref[0, :, c0:c0 + MXU_COLS] = _dot(h, w_ref[:, base + c0:base + c0 + MXU_COLS]).astype(BF16)


def _inproj_odd(x, sc, sh, w, q_norm, kv_norm, wq, wk, wv, cos, sin, tile):
    B, S, _ = x.shape
    row = lambda b, i: (b, i, 0)
    vec = lambda b, i: (b, 0, 0)
    full = lambda b, i: (0, 0)
    widths = (1024, 1024, 1024, 512, 512, 512)
    return pl.pallas_call(
        _inproj_odd_kernel,
        out_shape=[jax.ShapeDtypeStruct((B, S, wd), BF16) for wd in widths],
        grid=(B, S // tile),
        in_specs=[pl.BlockSpec((1, tile, D_MODEL), row),
                  pl.BlockSpec((1, 1, D_MODEL), vec), pl.BlockSpec((1, 1, D_MODEL), vec),
                  pl.BlockSpec(w.shape, full),
                  pl.BlockSpec((1, MLA_Q_RANK), full), pl.BlockSpec((1, MLA_KV_RANK), full),
                  pl.BlockSpec(wq.shape, full), pl.BlockSpec(wk.shape, full), pl.BlockSpec(wv.shape, full),
                  pl.BlockSpec((tile, LANES), lambda b, i: (i, 0)),
                  pl.BlockSpec((tile, LANES), lambda b, i: (i, 0))],
        out_specs=[pl.BlockSpec((1, tile, wd), row) for wd in widths],
        compiler_params=_cparams(("arbitrary", "arbitrary"), 48 * 1024 * 1024),
        name="inproj_odd",
    )(x, sc, sh, w, q_norm.reshape(1, -1), kv_norm.reshape(1, -1), wq, wk, wv, cos, sin)


def _mla_attn_kernel(q_ref, k_ref, v_ref, o_ref, m_ref, acc_ref, *, tk, unroll):
    _flash_sweep([q_ref[0]], k_ref, v_ref, m_ref, acc_ref, tk, unroll)
    acc = acc_ref[0]
    o_ref[0] = (acc / acc[:, MLA_VDIM:MLA_VDIM + 1]).astype(BF16)


def _mla_attention(qm, km, vm, tq, tk, unroll):
    B, S, _ = qm.shape
    return pl.pallas_call(
        functools.partial(_mla_attn_kernel, tk=tk, unroll=unroll),
        out_shape=jax.ShapeDtypeStruct((B, S, MLA_HEADS * LANES), BF16),
        grid=(B, MLA_HEADS, S // tq),
        in_specs=[pl.BlockSpec((1, tq, LANES), lambda b, h, i: (b, i, h)),
                  pl.BlockSpec((1, S, LANES), lambda b, h, i: (b, 0, h)),
                  pl.BlockSpec((1, S, LANES), lambda b, h, i: (b, 0, h))],
        out_specs=pl.BlockSpec((1, tq, LANES), lambda b, h, i: (b, i, h)),
        scratch_shapes=[pltpu.VMEM((1, tq, LANES), F32)] * 2,
        compiler_params=_cparams(("arbitrary",) * 3, 48 * 1024 * 1024),
        name="mla_attn",
    )(qm, km, vm)


def _na_bias_table(rpb):
    nq, nk = NA_QROWS * GRID_W, NA_KROWS * GRID_W
    qc, kc = np.arange(GRID_W)[:, None], np.arange(GRID_W)[None, :]
    qcs = np.clip(qc - NA_KC // 2, 0, GRID_W - NA_KC)
    col_ok = (kc >= qcs) & (kc < qcs + NA_KC)
    dc = np.clip(kc - qc + (NA_KC - 1), 0, 2 * NA_KC - 2)
    col_sel = (dc[:, :, None] == np.arange(2 * NA_KC - 1)) & col_ok[:, :, None]
    qr, kr = np.arange(NA_QROWS)[:, None], np.arange(NA_KROWS)[None, :]
    row_sel = []
    for pat in range(3):
        rel_r = qr + 4 * pat
        w0 = np.clip(rel_r - NA_KR // 2, 0, NA_KROWS - NA_KR)
        row_ok = (kr >= w0) & (kr < w0 + NA_KR)
        dr = kr - rel_r + (NA_KR - 1)
        row_sel.append((dr[:, :, None] == np.arange(2 * NA_KR - 1)) & row_ok[:, :, None])
    row_sel = np.stack(row_sel)
    tab = jnp.einsum("prsa,hab,ctb->phrcst", row_sel.astype(np.float32), rpb.astype(F32),
                     col_sel.astype(np.float32), precision=lax.Precision.HIGHEST)
    valid = row_sel.any(-1)[:, None, :, None, :, None] & col_ok[None, None, None, :, None, :]
    return jnp.where(valid, tab, NEG_INF).reshape(3, rpb.shape[0], nq, nk)


def _na_kernel(q_ref, k_ref, v_ref, bias_ref, o_ref, *, rows):
    nq, nk = NA_QROWS * GRID_W, NA_KROWS * GRID_W
    blk = pl.program_id(2)
    k0 = jnp.clip(blk * NA_QROWS - NA_KR // 2, 0, rows - NA_KROWS) * GRID_W
    k0 = pl.multiple_of(k0, GRID_W)
    k = k_ref[0, pl.ds(k0, nk), :]
    v = v_ref[0, pl.ds(k0, nk), :]
    q = q_ref[0]
    lane = lax.broadcasted_iota(I32, q.shape, 1)
    zero = jnp.zeros_like(q)
    outs = []
    for hh in range(2):
        qh = jnp.where((lane < NA_DH) == (hh == 0), q, zero)
        s = _dot_nt(qh, k) + bias_ref[0, hh]
        m = jnp.max(s, axis=1, keepdims=True)
        e = jnp.exp(s - m)
        p = e / jnp.sum(e, axis=1, keepdims=True)
        outs.append(_dot(p.astype(BF16), v))
    o_ref[0] = jnp.where(lane < NA_DH, outs[0], outs[1]).astype(BF16)


def _na_attention(nq_, nk_, nv_, bias):
    B, S, _ = nq_.shape
    rows = S // GRID_W
    nblk = rows // NA_QROWS
    nq, nk = NA_QROWS * GRID_W, NA_KROWS * GRID_W

    def pat(b, p, i):
        return (jnp.where(i == 0, 0, jnp.where(i == nblk - 1, 2, 1)), p, 0, 0)

    return pl.pallas_call(
        functools.partial(_na_kernel, rows=rows),
        out_shape=jax.ShapeDtypeStruct((B, S, NA_HEADS * NA_DH), BF16),
        grid=(B, NA_HEADS // 2, nblk),
        in_specs=[pl.BlockSpec((1, nq, LANES), lambda b, p, i: (b, i, p)),
                  pl.BlockSpec((1, S, LANES), lambda b, p, i: (b, 0, p)),
                  pl.BlockSpec((1, S, LANES), lambda b, p, i: (b, 0, p)),
                  pl.BlockSpec((1, 2, nq, nk), pat)],
        out_specs=pl.BlockSpec((1, nq, LANES), lambda b, p, i: (b, i, p)),
        compiler_params=_cparams(("arbitrary",) * 3, 48 * 1024 * 1024),
        name="na_attn",
    )(nq_, nk_, nv_, bias)


def _outproj_kernel(oa_ref, ob_ref, wa_ref, wb_ref, x_ref, g1_ref, lng_ref, lnb_ref, sc2_ref, sh2_ref,
                    rwh_ref, rwl_ref, x1_ref, h2_ref, aff_ref):
    y = _dot(oa_ref[0], wa_ref[...]) + _dot(ob_ref[0], wb_ref[...])
    x1 = _layer_norm(DEEPNORM_ALPHA * x_ref[0] + g1_ref[0] * y, lng_ref[...], lnb_ref[...])
    x1_ref[0] = x1
    h2 = x1 * (1.0 + sc2_ref[0]) + sh2_ref[0]
    h2_ref[0] = h2
    hh = h2.astype(BF16)
    hl = (h2 - hh.astype(F32)).astype(BF16)
    rwh, rwl = rwh_ref[...], rwl_ref[...]
    lg = _dot_nt(rwh, hh) + (_dot_nt(rwl, hh) + _dot_nt(rwh, hl))
    e = jnp.exp(lg - jnp.max(lg, axis=0, keepdims=True))
    aff_ref[...] = e / jnp.sum(e, axis=0, keepdims=True)


def _outproj(oa, ob, wa, wb, x, g1, lng, lnb, sc2, sh2, rwh, rwl, tile):
    B, S, _ = x.shape
    nt = S // tile
    row = lambda b, i: (b, i, 0)
    vec = lambda b, i: (b, 0, 0)
    full = lambda b, i: (0, 0)
    return pl.pallas_call(
        _outproj_kernel,
        out_shape=[jax.ShapeDtypeStruct((B, S, D_MODEL), F32), jax.ShapeDtypeStruct((B, S, D_MODEL), F32),
                   jax.ShapeDtypeStruct((N_EXPERTS, B * S), F32)],
        grid=(B, nt),
        in_specs=[pl.BlockSpec((1, tile, oa.shape[2]), row), pl.BlockSpec((1, tile, ob.shape[2]), row),
                  pl.BlockSpec(wa.shape, full), pl.BlockSpec(wb.shape, full),
                  pl.BlockSpec((1, tile, D_MODEL), row), pl.BlockSpec((1, 1, D_MODEL), vec),
                  pl.BlockSpec((1, D_MODEL), full), pl.BlockSpec((1, D_MODEL), full),
                  pl.BlockSpec((1, 1, D_MODEL), vec), pl.BlockSpec((1, 1, D_MODEL), vec),
                  pl.BlockSpec((N_EXPERTS, D_MODEL), full), pl.BlockSpec((N_EXPERTS, D_MODEL), full)],
        out_specs=[pl.BlockSpec((1, tile, D_MODEL), row), pl.BlockSpec((1, tile, D_MODEL), row),
                   pl.BlockSpec((N_EXPERTS, tile), lambda b, i: (0, b * nt + i))],
        compiler_params=_cparams(("arbitrary", "arbitrary"), 48 * 1024 * 1024),
        name="outproj_ln_router",
    )(oa, ob, wa, wb, x, g1, lng.reshape(1, -1), lnb.reshape(1, -1), sc2, sh2, rwh, rwl)


def _select_kernel(aff_ref, sel_ref, pos_ref, off_ref, *, cap):
    E, NT, _ = aff_ref.shape
    keys = lax.bitcast_convert_type(aff_ref[...], I32)

    def count_ge(c):
        hit = jnp.where(keys >= c, 1.0, 0.0)
        return jnp.sum(jnp.sum(hit, axis=2, keepdims=True), axis=1, keepdims=True)

    def bit_body(t, res):
        cand = res | (jnp.int32(1) << (30 - t))
        return jnp.where(count_ge(cand) >= cap, cand, res)

    thr = lax.fori_loop(0, 31, bit_body, jnp.zeros((E, 1, 1), I32))

    upper = jnp.where(lax.broadcasted_iota(I32, (LANES, LANES), 0) < lax.broadcasted_iota(I32, (LANES, LANES), 1),
                      1.0, 0.0).astype(BF16)
    lower = jnp.where(lax.broadcasted_iota(I32, (NT, NT), 1) < lax.broadcasted_iota(I32, (NT, NT), 0),
                      1.0, 0.0).astype(BF16)

    def excl_prefix(m):
        loc = _dot(m.astype(BF16), upper)
        cnt = jnp.broadcast_to(jnp.sum(m, axis=1, keepdims=True), m.shape).astype(BF16)
        start = _dot(lower, cnt)
        return start + loc, start

    for e in range(E):
        ke, te = keys[e], thr[e]
        gt = jnp.where(ke > te, 1.0, 0.0)
        eq = jnp.where(ke == te, 1.0, 0.0)
        need = cap - jnp.sum(jnp.sum(gt, axis=1, keepdims=True), axis=0, keepdims=True)
        tie_rank, _ = excl_prefix(eq)
        sel = jnp.maximum(gt, jnp.where(tie_rank < need, eq, 0.0))
        pos, start = excl_prefix(sel)
        sel_ref[e] = sel
        pos_ref[e] = pos
        off_ref[e] = start


def _select(aff3, cap):
    shp = jax.ShapeDtypeStruct(aff3.shape, F32)
    return pl.pallas_call(
        functools.partial(_select_kernel, cap=cap),
        out_shape=[shp, shp, shp],
        compiler_params=_cparams(None, 48 * 1024 * 1024),
        name="ec_select",
    )(aff3)


def _compact_kernel(offs_ref, aff_ref, sel_ref, pos_ref, flat_ref, *, tb, nt):
    i = pl.program_id(0)

    @pl.when(i == 0)
    def _():
        flat_ref[...] = jnp.zeros(flat_ref.shape, F32)

    sub = lax.broadcasted_iota(I32, (2 * LANES, LANES), 0).astype(F32)
    row = lax.broadcasted_iota(I32, (8, LANES), 0)
    lane = lax.broadcasted_iota(I32, (1, LANES), 1)
    for j in range(tb):
        tile = i * tb + j
        tok = tile * LANES + lane
        t_hi = (tok >> 8).astype(F32)
        t_lo = (tok & 255).astype(F32)
        for e in range(N_EXPERTS):
            q = offs_ref[e * nt + tile] >> 7
            g = aff_ref[e, j:j + 1, :]
            g_hi = g.astype(BF16).astype(F32)
            g_mid = (g - g_hi).astype(BF16).astype(F32)
            g_lo = g - g_hi - g_mid
            vals = jnp.where(row == 0, t_hi, jnp.where(row == 1, t_lo, jnp.where(
                row == 2, g_hi, jnp.where(row == 3, g_mid, jnp.where(row == 4, g_lo, 0.0)))))
            prel = jnp.where(sel_ref[e, j:j + 1, :] > 0.0,
                             pos_ref[e, j:j + 1, :] - (q * LANES).astype(F32), -1.0)
            onehot = jnp.where(sub == prel, 1.0, 0.0).astype(BF16)
            res = _dot_nt(vals.astype(BF16), onehot)
            flat_ref[e, q] += res[:, :LANES]
            flat_ref[e, q + 1] += res[:, LANES:]


def _compact(offs, aff3, sel3, pos3, cap, tb):
    E, NT, _ = aff3.shape
    nq = cap // LANES
    blk = pl.BlockSpec((E, tb, LANES), lambda i, offs: (0, i, 0))
    return pl.pallas_call(
        functools.partial(_compact_kernel, tb=tb, nt=NT),
        out_shape=jax.ShapeDtypeStruct((E, nq + 2, 8, LANES), F32),
        grid_spec=pltpu.PrefetchScalarGridSpec(
            num_scalar_prefetch=1, grid=(NT // tb,),
            in_specs=[blk, blk, blk],
            out_specs=pl.BlockSpec((E, nq + 2, 8, LANES), lambda i, offs: (0, 0, 0, 0))),
        compiler_params=_cparams(("arbitrary",)),
        name="ec_compact",
    )(offs, aff3, sel3, pos3)


GATHER_UNROLL = 8


def _ffn_kernel(idx_ref, h2_hbm, gate_ref, wg_ref, wu_ref, wd_ref, y_ref, xbuf, hbuf, sem, *, m, fb):
    nsteps = pl.num_programs(0) * pl.num_programs(1)
    step = pl.program_id(0) * pl.num_programs(1) + pl.program_id(1)
    slot = step % 2

    def row_copy(tok, sl, r):
        return pltpu.make_async_copy(h2_hbm.at[pl.ds(tok, 1), :], xbuf.at[sl, pl.ds(r, 1), :], sem.at[sl])

    def issue(st, sl):
        def body(g, c):
            for u in range(GATHER_UNROLL):
                r = g * GATHER_UNROLL + u
                row_copy(idx_ref[st * m + r], sl, r).start()
            return c
        lax.fori_loop(0, m // GATHER_UNROLL, body, 0)

    @pl.when(step == 0)
    def _():
        issue(0, 0)

    @pl.when(step + 1 < nsteps)
    def _():
        issue(step + 1, 1 - slot)

    pltpu.make_async_copy(h2_hbm.at[pl.ds(0, m), :], xbuf.at[slot], sem.at[slot]).wait()

    x = xbuf[slot].astype(BF16)
    nf = wg_ref.shape[2] // fb
    for f in range(nf):
        a = _dot(x, wg_ref[0, :, f * fb:(f + 1) * fb])
        u = _dot(x, wu_ref[0, :, f * fb:(f + 1) * fb])
        hbuf[:, f * fb:(f + 1) * fb] = (a * jax.nn.sigmoid(a) * u).astype(BF16)
    y = _dot(hbuf[...], wd_ref[0])
    eye = lax.broadcasted_iota(I32, (LANES, LANES), 0) == lax.broadcasted_iota(I32, (LANES, LANES), 1)
    cols = [jnp.sum(jnp.where(eye, gate_ref[0, r:r + 1, :], 0.0), axis=1, keepdims=True)
            for r in range(m // LANES)]
    y_ref[...] = (y * jnp.concatenate(cols, axis=0)).astype(BF16)


def _expert_ffn(idx, h2, gates, wg, wu, wd, cap, m):
    E = N_EXPERTS
    nc = cap // m
    f = wg.shape[2]
    return pl.pallas_call(
        functools.partial(_ffn_kernel, m=m, fb=256),
        out_shape=jax.ShapeDtypeStruct((E * cap, D_MODEL), BF16),
        grid_spec=pltpu.PrefetchScalarGridSpec(
            num_scalar_prefetch=1, grid=(E, nc),
            in_specs=[pl.BlockSpec(memory_space=pl.ANY),
                      pl.BlockSpec((1, m // LANES, LANES), lambda e, c, idx: (e * nc + c, 0, 0)),
                      pl.BlockSpec((1, D_MODEL, f), lambda e, c, idx: (e, 0, 0)),
                      pl.BlockSpec((1, D_MODEL, f), lambda e, c, idx: (e, 0, 0)),
                      pl.BlockSpec((1, f, D_MODEL), lambda e, c, idx: (e, 0, 0))],
            out_specs=pl.BlockSpec((m, D_MODEL), lambda e, c, idx: (e * nc + c, 0)),
            scratch_shapes=[pltpu.VMEM((2, m, D_MODEL), F32), pltpu.VMEM((m, f), BF16),
                            pltpu.SemaphoreType.DMA((2,))]),
        compiler_params=_cparams(("arbitrary", "arbitrary"), VMEM_LIMIT_V7X),
        name="expert_ffn",
    )(idx, h2, gates.reshape(E * nc, m // LANES, LANES), wg, wu, wd)


COMBINE_WIN = 128


def _combine_kernel(offc_ref, x1_ref, g2_ref, lng_ref, lnb_ref, pos_ref, y_hbm, o_ref, ybuf, xtra, sem, xsem, acc_ref,
                    *, cap, ntc):
    E, W = N_EXPERTS, COMBINE_WIN
    i = pl.program_id(0)
    nsteps = pl.num_programs(0)
    slot = i % 2

    def first_window(e, step):
        a0 = offc_ref[e * (ntc + 1) + step]
        return pl.multiple_of(jnp.minimum((a0 >> 3) << 3, cap - W), 8)

    def window_copy(e, w, sl):
        return pltpu.make_async_copy(y_hbm.at[pl.ds(e * cap + w, W), :], ybuf.at[sl, pl.ds(e * W, W), :], sem.at[sl])

    def issue(step, sl):
        for e in range(E):
            window_copy(e, first_window(e, step), sl).start()

    @pl.when(i == 0)
    def _():
        issue(0, 0)

    @pl.when(i + 1 < nsteps)
    def _():
        issue(i + 1, 1 - slot)

    pos = pos_ref[...]
    lane = lax.broadcasted_iota(I32, (1, W), 1).astype(F32)
    ps = []
    for e in range(E):
        rel = pos[:, e:e + 1] - first_window(e, i).astype(F32)
        ps.append(jnp.where(rel == lane, 1.0, 0.0).astype(BF16))
    for e in range(E):
        window_copy(e, 0, slot).wait()
    acc_ref[...] = _dot(jnp.concatenate(ps, axis=1), ybuf[slot])

    for e in range(E):
        w0 = first_window(e, i)
        left = offc_ref[e * (ntc + 1) + i + 1] - (w0 + W)
        nextra = jnp.where(left > 0, (left + (W - 1)) // W, 0)

        def extra(kk, c, e=e, w0=w0):
            wu = w0 + (kk + 1) * W
            wc = pl.multiple_of(jnp.minimum(wu, cap - W), 8)
            cp = pltpu.make_async_copy(y_hbm.at[pl.ds(e * cap + wc, W), :], xtra, xsem)
            cp.start()
            cp.wait()
            pe = pos[:, e:e + 1]
            rel = jnp.where(pe >= wu.astype(F32), pe - wc.astype(F32), -1.0)
            acc_ref[...] += _dot(jnp.where(rel == lane, 1.0, 0.0).astype(BF16), xtra[...])
            return c
        lax.fori_loop(0, nextra, extra, 0)

    z = DEEPNORM_ALPHA * x1_ref[...] + g2_ref[0] * acc_ref[...]
    o_ref[...] = _layer_norm(z, lng_ref[...], lnb_ref[...])


def _combine(offc, x1, g2, lng, lnb, pos_t, y, cap, tc, tiles_per_seq):
    n = x1.shape[0]
    ntc = n // tc
    E, W = N_EXPERTS, COMBINE_WIN
    return pl.pallas_call(
        functools.partial(_combine_kernel, cap=cap, ntc=ntc),
        out_shape=jax.ShapeDtypeStruct((n, D_MODEL), F32),
        grid_spec=pltpu.PrefetchScalarGridSpec(
            num_scalar_prefetch=1, grid=(ntc,),
            in_specs=[pl.BlockSpec((tc, D_MODEL), lambda i, o: (i, 0)),
                      pl.BlockSpec((1, 1, D_MODEL), lambda i, o: (i // tiles_per_seq, 0, 0)),
                      pl.BlockSpec((1, D_MODEL), lambda i, o: (0, 0)),
                      pl.BlockSpec((1, D_MODEL), lambda i, o: (0, 0)),
                      pl.BlockSpec((tc, E), lambda i, o: (i, 0)),
                      pl.BlockSpec(memory_space=pl.ANY)],
            out_specs=pl.BlockSpec((tc, D_MODEL), lambda i, o: (i, 0)),
            scratch_shapes=[pltpu.VMEM((2, E * W, D_MODEL), BF16), pltpu.VMEM((W, D_MODEL), BF16),
                            pltpu.SemaphoreType.DMA((2,)), pltpu.SemaphoreType.DMA,
                            pltpu.VMEM((tc, D_MODEL), F32)]),
        compiler_params=_cparams(("arbitrary",), 48 * 1024 * 1024),
        name="ec_combine_ln",
    )(offc, x1, g2, lng.reshape(1, -1), lnb.reshape(1, -1), pos_t, y)


def _rope_tables(seq, dim, lo, period):
    inv = 1.0 / (ROPE_THETA ** (jnp.arange(0, dim, 2, dtype=F32) / dim))
    ang = jnp.arange(seq, dtype=F32)[:, None] * inv[None, :]
    lane = jnp.arange(LANES)
    rel = (lane - lo) % period
    on = (lane >= lo) & (rel < dim)
    j = rel % (dim // 2)
    cos = jnp.where(on[None, :], jnp.cos(ang)[:, j], 1.0)
    sgn = jnp.where(rel < dim // 2, -1.0, 1.0)
    sin = jnp.where(on[None, :], jnp.sin(ang)[:, j] * sgn[None, :], 0.0)
    return cos, sin


def _split_bf16(w):
    hi = w.astype(BF16)
    return hi, (w - hi.astype(F32)).astype(BF16)


def _moe(x1, h2, affT, g2, lng, lnb, wg, wu, wd, B, S):
    n = B * S
    E = N_EXPERTS
    cap = EC_CAPACITY_FACTOR * n // E
    nt = n // LANES
    aff3 = affT.reshape(E, nt, LANES)
    sel3, pos3, off3 = _select(aff3, cap)
    offs = off3[:, :, 0].astype(I32)
    flat = _compact(offs.reshape(-1), aff3, sel3, pos3, cap, min(8, nt))
    nq = cap // LANES
    idx3 = (flat[:, :nq, 0, :] * 256.0 + flat[:, :nq, 1, :]).astype(I32)
    gates = flat[:, :nq, 2, :] + flat[:, :nq, 3, :] + flat[:, :nq, 4, :]
    m = min(512, cap)
    y = _expert_ffn(idx3.reshape(-1), h2.reshape(n, D_MODEL), gates, wg, wu, wd, cap, m)
    tc = 512
    stride = tc // LANES
    offc = jnp.concatenate([offs[:, ::stride], jnp.full((E, 1), cap, I32)], axis=1)
    pos_t = jnp.where(sel3 > 0.0, pos3, -1.0).reshape(E, n).T
    return _combine(offc.reshape(-1), x1.reshape(n, D_MODEL), g2, lng, lnb, pos_t, y, cap, tc, S // tc)


def _trunk(x, mod, W):
    B, S, _ = x.shape
    tile = 512
    diff_tiles = (min(2048, S), min(512, S), 2)
    mla_tiles = (min(1024, S), min(2048, S), 1)
    cos_h, sin_h = _rope_tables(S, HEAD_DIM, 0, HEAD_DIM)
    cos_r, sin_r = _rope_tables(S, MLA_ROPE, MLA_NOPE, LANES)
    for l in range(DEPTH):
        sh1, sc1, g1, sh2, sc2, g2 = [mod[l][:, None, k * D_MODEL:(k + 1) * D_MODEL] for k in range(6)]
        if l % 2 == 0:
            e = l // 2
            lam_init = 0.8 - 0.6 * math.exp(-0.3 * l)
            qa, ka, va, qbp, kb, vb, vbs = _inproj_even(x, sc1, sh1, W["w_in_even"][e], cos_h, sin_h, tile)
            oa = _diff_attention(qa, ka, va, W["diff_lambda"][e], W["diff_subln"][e], lam_init, *diff_tiles)
            ob = _window_attention(qbp, kb, vb, vbs, W["swa_sink"][e])
            wa, wb = W["w_out_even"][e]
        else:
            o = l // 2
            qm, km, vm, nq_, nk_, nv_ = _inproj_odd(x, sc1, sh1, W["w_in_odd"][o], W["mla_q_norm"][o],
                                                    W["mla_kv_norm"][o], *W["mla_up"][o], cos_r, sin_r, tile)
            oa = _mla_attention(qm, km, vm, *mla_tiles)
            ob = _na_attention(nq_, nk_, nv_, W["na_bias"][o])
            wa, wb = W["w_out_odd"][o]
        x1, h2, affT = _outproj(oa, ob, wa, wb, x, g1, W["ln_g"][l, 0], W["ln_b"][l, 0], sc2, sh2,
                                *W["router"][l], tile)
        x = _moe(x1, h2, affT, g2, W["ln_g"][l, 1], W["ln_b"][l, 1],
                 W["exp_w_gate"][l], W["exp_w_up"][l], W["exp_w_down"][l], B, S).reshape(B, S, D_MODEL)
    return x


def kernel(x_prompt, x_sample, c_prompt, c_sample, w_in_even, diff_lambda, diff_subln, swa_sink, w_out_even,
           w_in_odd, mla_q_norm, mla_w_uq, mla_kv_norm, mla_w_ukv, na_rpb, w_out_odd, ada_w, ada_b, ln_g, ln_b,
           router_w, exp_w_gate, exp_w_up, exp_w_down):
    bp, bs = x_prompt.shape[0], x_sample.shape[0]
    c_all = jnp.concatenate([c_prompt, c_sample, jnp.zeros((8 - bp - bs, D_MODEL), F32)], axis=0)
    mod = _modulation(c_all, ada_w, ada_b)
    n_even, n_odd = w_in_even.shape[0], w_in_odd.shape[0]
    ka = DIFF_HEADS * 2 * DIFF_DH
    zpad = jnp.zeros((MLA_HEADS, LANES - MLA_VDIM, D_MODEL), F32)

    def pad_mla_rows(w):
        return jnp.concatenate([w.reshape(MLA_HEADS, MLA_VDIM, D_MODEL), zpad], axis=1).reshape(-1, D_MODEL)

    km = MLA_HEADS * MLA_VDIM
    W = {
        "w_in_even": [_prep_w_in_even(w_in_even[e]) for e in range(n_even)],
        "diff_lambda": diff_lambda, "diff_subln": diff_subln, "swa_sink": swa_sink,
        "w_out_even": [(w_out_even[e, :ka].astype(BF16), w_out_even[e, ka:].astype(BF16)) for e in range(n_even)],
        "w_in_odd": [_prep_w_in_odd(w_in_odd[o]) for o in range(n_odd)],
        "mla_q_norm": mla_q_norm, "mla_kv_norm": mla_kv_norm,
        "mla_up": [_prep_mla_up(mla_w_uq[o], mla_w_ukv[o]) for o in range(n_odd)],
        "na_bias": [_na_bias_table(na_rpb[o]) for o in range(n_odd)],
        "w_out_odd": [(pad_mla_rows(w_out_odd[o, :km]).astype(BF16), w_out_odd[o, km:].astype(BF16))
                      for o in range(n_odd)],
        "ln_g": ln_g, "ln_b": ln_b,
        "router": [_split_bf16(router_w[l].T) for l in range(DEPTH)],
        "exp_w_gate": exp_w_gate.astype(BF16), "exp_w_up": exp_w_up.astype(BF16),
        "exp_w_down": exp_w_down.astype(BF16),
    }
    y_prompt = _trunk(x_prompt, mod[:, 0:bp], W)
    y_sample = _trunk(x_sample, mod[:, bp:bp + bs], W)
    return (y_prompt, y_sample)
```
